```python
import jax, jax.numpy as jnp
from jax import lax
import numpy as np

D_MODEL = 1024
BATCH = 16
SEQ = 2048
DEPTH = 4

GRID_W = 64
CTX_LEN = 256
N_MIXERS = 4
N_MOD = 6
EPS = 1e-6
NEG_INF = -1e30
D_FF = 4 * D_MODEL
ROPE_BASE = 10000.0
RET_HEADS = 8
RET_DK = D_MODEL // RET_HEADS
RET_DV = 2 * RET_DK
RET_CHUNK = 128
RET_IN = 2 * RET_HEADS * RET_DK + 3 * RET_HEADS * RET_DV
ATT_HEADS = 16
ATT_KV_HEADS = 4
ATT_DH = D_MODEL // ATT_HEADS
ATT_WINDOW = 128
ATT_BLOCK = 128
ATT_IN = (ATT_HEADS + 2 * ATT_KV_HEADS) * ATT_DH
POOL_WINDOWS = (2, 4, 8, 16)
POOL_GROUP = D_MODEL // len(POOL_WINDOWS)
DN_HEADS = 8
DN_DK = D_MODEL // DN_HEADS
DN_DV = DN_DK
DN_CONV_W = 5
DN_CHUNK = 64
DN_QKV = 2 * DN_HEADS * DN_DK + DN_HEADS * DN_DV
DN_IN = DN_QKV + 4 * DN_HEADS + 2 * DN_HEADS * DN_DV

kernel_name = 'hybrid_interleaved_dit_trunk'


def _n_uses(m):
    return len(range(m, DEPTH, N_MIXERS))


def rmsnorm(x, g):
    xf = x.astype(jnp.float32)
    y = xf * lax.rsqrt(jnp.mean(xf * xf, axis=-1, keepdims=True) + EPS)
    return (y * g.astype(jnp.float32)).astype(x.dtype)


def head_norm(o, dtype, g=None):
    of = o.astype(jnp.float32)
    y = of * lax.rsqrt(jnp.mean(of * of, axis=-1, keepdims=True) + EPS)
    if g is not None:
        y = y * g.astype(jnp.float32)
    return y.astype(dtype)


def l2norm(t):
    tf = t.astype(jnp.float32)
    return tf * lax.rsqrt(jnp.sum(tf * tf, axis=-1, keepdims=True) + EPS)


def grid_positions(n):
    rows = n // GRID_W
    row = jnp.broadcast_to(jnp.arange(rows, dtype=jnp.int32)[:, None], (rows, GRID_W)).reshape(-1)
    col = jnp.broadcast_to(jnp.arange(GRID_W, dtype=jnp.int32)[None, :], (rows, GRID_W)).reshape(-1)
    return row, col


def _rotate(x, pos):
    half = x.shape[-1] // 2
    inv = ROPE_BASE ** (-jnp.arange(half, dtype=jnp.float32) / half)
    ang = pos.astype(jnp.float32)[:, None] * inv[None, :]
    cos, sin = jnp.cos(ang).astype(x.dtype), jnp.sin(ang).astype(x.dtype)
    x1, x2 = x[..., :half], x[..., half:]
    return jnp.concatenate([x1 * cos - x2 * sin, x2 * cos + x1 * sin], axis=-1)


def axial_rope(x, row, col):
    h = x.shape[-1] // 2
    return jnp.concatenate([_rotate(x[..., :h], row), _rotate(x[..., h:], col)], axis=-1)


def to_heads(t, n_heads):
    b, n, hd = t.shape
    return t.reshape(b, n, n_heads, hd // n_heads).transpose(0, 2, 1, 3)


def from_heads(t):
    b, h, n, d = t.shape
    return t.transpose(0, 2, 1, 3).reshape(b, n, h * d)


def flip_seq(t):
    return jnp.flip(t, axis=2)


def sink_softmax(scores, sink):
    s = sink[:, :, None, None]
    m = jnp.maximum(jnp.max(scores, axis=-1, keepdims=True), s)
    p = jnp.exp(scores - m)
    return p / (jnp.sum(p, axis=-1, keepdims=True) + jnp.exp(s - m))


def centred_mean(x, w):
    n = x.shape[1]
    lo_off = w // 2
    hi_off = w - 1 - lo_off
    cs = jnp.pad(jnp.cumsum(x.astype(jnp.float32), axis=1), ((0, 0), (1, 0), (0, 0)))
    t = jnp.arange(n)
    lo = jnp.clip(t - lo_off, 0, n)
    hi = jnp.clip(t + hi_off + 1, 0, n)
    cnt = (hi - lo).astype(jnp.float32)[None, :, None]
    return ((cs[:, hi] - cs[:, lo]) / cnt).astype(x.dtype)


def short_conv(x, w):
    k = w.shape[0]
    pad = k // 2
    y = lax.conv_general_dilated(x, w[:, None, :].astype(x.dtype), window_strides=(1,),
                                 padding=[(pad, k - 1 - pad)], dimension_numbers=('NWC', 'WIO', 'NWC'),
                                 feature_group_count=x.shape[-1])
    return jax.nn.silu(y)


def retention_scan(q, k, v, log_gamma, s0):
    b, h, n, _ = q.shape
    dv = v.shape[-1]
    c = RET_CHUNK
    nc = n // c

    def chunks(t):
        return jnp.moveaxis(t.astype(jnp.float32).reshape(b, h, nc, c, t.shape[-1]), 2, 0)

    qc, kc, vc = chunks(q), chunks(k), chunks(v)
    idx = jnp.arange(c, dtype=jnp.float32)
    lg = log_gamma.astype(jnp.float32)[:, None]
    diff = idx[:, None] - idx[None, :]
    decay_in = jnp.where(diff >= 0, jnp.exp(lg[..., None] * jnp.maximum(diff, 0.0)), 0.0)
    q_decay = jnp.exp(lg * (idx + 1.0))
    k_decay = jnp.exp(lg * (c - 1.0 - idx))
    chunk_decay = jnp.exp(lg[:, 0] * c)

    def step(s, inp):
        qi, ki, vi = inp
        scores = jnp.einsum('bhid,bhjd->bhij', qi, ki) * decay_in
        o = jnp.einsum('bhij,bhjv->bhiv', scores, vi) + jnp.einsum('bhid,bhdv->bhiv', qi * q_decay[..., None], s)
        s = s * chunk_decay[:, None, None] + jnp.einsum('bhjd,bhjv->bhdv', ki * k_decay[..., None], vi)
        return s, o

    s_fin, oc = lax.scan(step, s0, (qc, kc, vc))
    return jnp.moveaxis(oc, 0, 2).reshape(b, h, n, dv), s_fin


def gated_delta_scan(q, k, v, beta, log_alpha, s0):
    b, h, n, dk = q.shape
    dv = v.shape[-1]
    c = DN_CHUNK
    nc = n // c

    def chunks(t):
        t = t.astype(jnp.float32)
        return t.reshape(b, h, nc, c, *t.shape[3:])

    q, k, v = chunks(q) * dk ** -0.5, chunks(k), chunks(v)
    beta = chunks(beta)
    g = jnp.cumsum(chunks(log_alpha), axis=-1)
    tri = jnp.tril(jnp.ones((c, c), dtype=bool))
    strict = jnp.tril(jnp.ones((c, c), dtype=bool), -1)
    gd = g[..., :, None] - g[..., None, :]
    decay = jnp.where(tri, jnp.exp(jnp.where(tri, gd, 0.0)), 0.0)
    kb = k * beta[..., None]
    a = jnp.where(strict, jnp.einsum('bhnid,bhnjd->bhnij', kb, k) * decay, 0.0)
    rhs = jnp.concatenate([v * beta[..., None], kb * jnp.exp(g)[..., None]], axis=-1)
    sol = lax.linalg.triangular_solve(a, rhs, left_side=True, lower=True, unit_diagonal=True)
    u, w = sol[..., :dv], sol[..., dv:]
    attn = jnp.einsum('bhnid,bhnjd->bhnij', q, k) * decay
    q_g = q * jnp.exp(g)[..., None]
    k_g = k * jnp.exp(g[..., -1:] - g)[..., None]
    g_last = jnp.exp(g[..., -1])
    xs = tuple(jnp.moveaxis(t, 2, 0) for t in (u, w, attn, q_g, k_g, g_last))

    def step(s, inp):
        u_i, w_i, attn_i, qg_i, kg_i, gl_i = inp
        v_new = u_i - jnp.einsum('bhid,bhdv->bhiv', w_i, s)
        o = jnp.einsum('bhid,bhdv->bhiv', qg_i, s) + jnp.einsum('bhij,bhjv->bhiv', attn_i, v_new)
        s = s * gl_i[..., None, None] + jnp.einsum('bhjd,bhjv->bhdv', kg_i, v_new)
        return s, o

    s_fin, o = lax.scan(step, s0, xs)
    return jnp.moveaxis(o, 0, 2).reshape(b, h, n, dv), s_fin


def retention_mixer(u_lat, u_ctx, w_in, decay_logit, w_out, row, col, with_ctx_out):
    hk, hv = RET_HEADS * RET_DK, RET_HEADS * RET_DV
    splits = [hk, 2 * hk, 2 * hk + hv, 2 * hk + 2 * hv]
    lg_f = jax.nn.log_sigmoid(decay_logit[0].astype(jnp.float32))
    lg_b = jax.nn.log_sigmoid(decay_logit[1].astype(jnp.float32))

    def project(u, rotate):
        q, k, v, g_f, g_b = jnp.split(u @ w_in, splits, axis=-1)
        q = to_heads(q, RET_HEADS) * RET_DK ** -0.5
        k = to_heads(k, RET_HEADS)
        if rotate:
            q, k = axial_rope(q, row, col), axial_rope(k, row, col)
        return q, k, to_heads(v, RET_HEADS), g_f, g_b

    def both_dirs(q, k, v, s0_f, s0_b):
        o_f, s_f = retention_scan(q, k, v, lg_f, s0_f)
        o_b, s_b = retention_scan(flip_seq(q), flip_seq(k), flip_seq(v), lg_b, s0_b)
        return o_f, flip_seq(o_b), s_f, s_b

    def merge(o_f, o_b, g_f, g_b, dtype):
        y = from_heads(head_norm(o_f, dtype)) * jax.nn.silu(g_f) + from_heads(head_norm(o_b, dtype)) * jax.nn.silu(g_b)
        return y @ w_out

    zeros = jnp.zeros((u_lat.shape[0], RET_HEADS, RET_DK, RET_DV), jnp.float32)
    qc, kc, vc, gfc, gbc = project(u_ctx, False)
    oc_f, oc_b, sc_f, sc_b = both_dirs(qc, kc, vc, zeros, zeros)
    ql, kl, vl, gfl, gbl = project(u_lat, True)
    ol_f, ol_b, _, _ = both_dirs(ql, kl, vl, sc_f, sc_b)
    y_lat = merge(ol_f, ol_b, gfl, gbl, u_lat.dtype)
    y_ctx = merge(oc_f, oc_b, gfc, gbc, u_ctx.dtype) if with_ctx_out else None
    return y_lat, y_ctx


def window_attention_mixer(u_lat, u_ctx, w_in, sink, w_out, row, col, with_ctx_out):
    b, n, _ = u_lat.shape
    grp = ATT_HEADS // ATT_KV_HEADS
    nq, nkv = ATT_HEADS * ATT_DH, ATT_KV_HEADS * ATT_DH
    sink_f = sink.astype(jnp.float32).reshape(ATT_KV_HEADS, grp)

    def project(u):
        q, k, v = jnp.split(u @ w_in, [nq, nq + nkv], axis=-1)
        q = q.reshape(b, u.shape[1], ATT_KV_HEADS, grp, ATT_DH).transpose(0, 2, 3, 1, 4) * ATT_DH ** -0.5
        return q, to_heads(k, ATT_KV_HEADS), to_heads(v, ATT_KV_HEADS)

    def merge(o):
        return o.transpose(0, 3, 1, 2, 4).reshape(b, o.shape[3], nq) @ w_out

    qc, kc, vc = project(u_ctx)
    ql, kl, vl = project(u_lat)
    ql, kl = axial_rope(ql, row, col), axial_rope(kl, row, col)

    span = ATT_BLOCK + 2 * ATT_WINDOW
    pad = ((0, 0), (0, 0), (ATT_WINDOW, ATT_WINDOW), (0, 0))
    kp, vp = jnp.pad(kl, pad), jnp.pad(vl, pad)
    qi = jnp.arange(ATT_BLOCK)[:, None]
    kj = jnp.arange(span)[None, :]
    in_window = jnp.abs(kj - ATT_WINDOW - qi) <= ATT_WINDOW

    def block(bi):
        start = bi * ATT_BLOCK
        qb = lax.dynamic_slice_in_dim(ql, start, ATT_BLOCK, axis=3)
        kb = lax.dynamic_slice_in_dim(kp, start, span, axis=2)
        vb = lax.dynamic_slice_in_dim(vp, start, span, axis=2)
        key_pos = start - ATT_WINDOW + kj
        valid = in_window & (key_pos >= 0) & (key_pos < n)
        s_loc = jnp.where(valid, jnp.einsum('bkgqd,bksd->bkgqs', qb, kb).astype(jnp.float32), NEG_INF)
        s_ctx = jnp.einsum('bkgqd,bksd->bkgqs', qb, kc).astype(jnp.float32)
        p = sink_softmax(jnp.concatenate([s_loc, s_ctx], axis=-1), sink_f).astype(vb.dtype)
        return (jnp.einsum('bkgqs,bksd->bkgqd', p[..., :span], vb)
                + jnp.einsum('bkgqs,bksd->bkgqd', p[..., span:], vc))

    o_blocks = lax.map(block, jnp.arange(n // ATT_BLOCK))
    o_lat = jnp.moveaxis(o_blocks, 0, 3).reshape(b, ATT_KV_HEADS, grp, n, ATT_DH)
    y_lat = merge(o_lat)
    if with_ctx_out:
        s_cc = jnp.einsum('bkgqd,bksd->bkgqs', qc, kc).astype(jnp.float32)
        p_cc = sink_softmax(s_cc, sink_f).astype(vc.dtype)
        y_ctx = merge(jnp.einsum('bkgqs,bksd->bkgqd', p_cc, vc))
    else:
        y_ctx = None
    return y_lat, y_ctx


def pool_mixer(u_lat, u_ctx, w_grp, b_grp, scale, with_ctx_out):
    def mix(u):
        b, n, _ = u.shape
        ug = u.reshape(b, n, len(POOL_WINDOWS), POOL_GROUP)
        diffs = jnp.stack([centred_mean(ug[:, :, gi], w) - ug[:, :, gi] for gi, w in enumerate(POOL_WINDOWS)], axis=2)
        y = jnp.einsum('bngc,gcd->bngd', diffs, w_grp).reshape(b, n, D_MODEL) + b_grp
        return y * scale

    return mix(u_lat), (mix(u_ctx) if with_ctx_out else None)


def deltanet_mixer(u_lat, u_ctx, w_in, conv_w, a_log, dt_bias, norm_g, w_out, with_ctx_out):
    nh, hk, hv = DN_HEADS, DN_HEADS * DN_DK, DN_HEADS * DN_DV
    splits = [DN_QKV, DN_QKV + nh, DN_QKV + 2 * nh, DN_QKV + 3 * nh, DN_QKV + 4 * nh, DN_QKV + 4 * nh + hv]

    def gates(bb, aa, d):
        beta = jax.nn.sigmoid(bb.astype(jnp.float32)).transpose(0, 2, 1)
        log_alpha = -jnp.exp(a_log[d].astype(jnp.float32)) * jax.nn.softplus(aa.astype(jnp.float32) + dt_bias[d].astype(jnp.float32))
        return beta, log_alpha.transpose(0, 2, 1)

    def project(u):
        qkv, b_f, b_b, a_f, a_b, g_f, g_b = jnp.split(u @ w_in, splits, axis=-1)
        qkv = short_conv(qkv, conv_w)
        q, k, v = jnp.split(qkv, [hk, 2 * hk], axis=-1)
        q, k, v = l2norm(to_heads(q, nh)), l2norm(to_heads(k, nh)), to_heads(v, nh)
        return q, k, v, gates(b_f, a_f, 0), gates(b_b, a_b, 1), g_f, g_b

    def both_dirs(q, k, v, gf, gb, s0_f, s0_b):
        o_f, s_f = gated_delta_scan(q, k, v, gf[0], gf[1], s0_f)
        o_b, s_b = gated_delta_scan(flip_seq(q), flip_seq(k), flip_seq(v), flip_seq(gb[0]), flip_seq(gb[1]), s0_b)
        return o_f, flip_seq(o_b), s_f, s_b

    def merge(o_f, o_b, g_f, g_b, dtype):
        y = (from_heads(head_norm(o_f, dtype, norm_g)) * jax.nn.silu(g_f)
             + from_heads(head_norm(o_b, dtype, norm_g)) * jax.nn.silu(g_b))
        return y @ w_out

    zeros = jnp.zeros((u_lat.shape[0], nh, DN_DK, DN_DV), jnp.float32)
    qc, kc, vc, gfc, gbc, ofc, obc = project(u_ctx)
    oc_f, oc_b, sc_f, sc_b = both_dirs(qc, kc, vc, gfc, gbc, zeros, zeros)
    ql, kl, vl, gfl, gbl, ofl, obl = project(u_lat)
    ol_f, ol_b, _, _ = both_dirs(ql, kl, vl, gfl, gbl, sc_f, sc_b)
    y_lat = merge(ol_f, ol_b, ofl, obl, u_lat.dtype)
    y_ctx = merge(oc_f, oc_b, ofc, obc, u_ctx.dtype) if with_ctx_out else None
    return y_lat, y_ctx


def ffn_sublayer(h, shift, scale, gate, pre_g, post_g, w1, w2):
    u = rmsnorm(h, pre_g) * (1 + scale) + shift
    y = jnp.square(jax.nn.relu(u @ w1)) @ w2
    return h + gate * rmsnorm(y, post_g)


def setup_inputs(seed: int = 0) -> dict:
    key = jax.random.key(seed)
    ks = iter(jax.random.split(key, 32))

    def nrm(shape, s):
        return jax.random.normal(next(ks), shape, jnp.float32) * s

    def unif(shape, lo, hi):
        return jax.random.uniform(next(ks), shape, jnp.float32, lo, hi)

    D = D_MODEL
    na, nb, nc, nd = (_n_uses(m) for m in range(N_MIXERS))
    gamma = 1.0 - 2.0 ** (-5.0 - jnp.arange(RET_HEADS, dtype=jnp.float32))
    ret_logit0 = jnp.log(gamma) - jnp.log(1.0 - gamma)
    dt = jnp.exp(unif((nd, 2, DN_HEADS), float(np.log(1e-3)), float(np.log(1e-1))))
    return {
        'x': nrm((BATCH, SEQ, D), 1.0),
        'c': nrm((BATCH, D), 1.0),
        'ctx': nrm((BATCH, CTX_LEN, D), 1.0),
        'c_ctx': nrm((D,), 1.0),
        'ada_w': nrm((DEPTH, D, N_MOD * D), 0.5 * D ** -0.5),
        'ada_b': nrm((DEPTH, N_MOD * D), 0.02),
        'mix_pre_g': 1.0 + nrm((DEPTH, D), 0.05),
        'mix_post_g': 1.0 + nrm((DEPTH, D), 0.05),
        'mlp_pre_g': 1.0 + nrm((DEPTH, D), 0.05),
        'mlp_post_g': 1.0 + nrm((DEPTH, D), 0.05),
        'mlp_w1': nrm((DEPTH, D, D_FF), D ** -0.5),
        'mlp_w2': nrm((DEPTH, D_FF, D), D_FF ** -0.5),
        'ret_w_in': nrm((na, D, RET_IN), D ** -0.5),
        'ret_decay_logit': ret_logit0[None, None, :] + nrm((na, 2, RET_HEADS), 0.1),
        'ret_w_out': nrm((na, RET_HEADS * RET_DV, D), (RET_HEADS * RET_DV) ** -0.5),
        'att_w_in': nrm((nb, D, ATT_IN), D ** -0.5),
        'att_sink': nrm((nb, ATT_HEADS), 0.5),
        'att_w_out': nrm((nb, ATT_HEADS * ATT_DH, D), (ATT_HEADS * ATT_DH) ** -0.5),
        'pool_w': nrm((nc, len(POOL_WINDOWS), POOL_GROUP, POOL_GROUP), POOL_GROUP ** -0.5),
        'pool_b': nrm((nc, D), 0.02),
        'pool_scale': 1.0 + nrm((nc, D), 0.1),
        'dn_w_in': nrm((nd, D, DN_IN), D ** -0.5),
        'dn_conv_w': nrm((nd, DN_CONV_W, DN_QKV), DN_CONV_W ** -0.5),
        'dn_a_log': jnp.log(unif((nd, 2, DN_HEADS), 1.0, 16.0)),
        'dn_dt_bias': dt + jnp.log(-jnp.expm1(-dt)),
        'dn_norm_g': 1.0 + nrm((nd, DN_DV), 0.05),
        'dn_w_out': nrm((nd, DN_HEADS * DN_DV, D), (DN_HEADS * DN_DV) ** -0.5),
    }


def reference(x, c, ctx, c_ctx, ada_w, ada_b, mix_pre_g, mix_post_g, mlp_pre_g, mlp_post_g, mlp_w1, mlp_w2,
              ret_w_in, ret_decay_logit, ret_w_out, att_w_in, att_sink, att_w_out,
              pool_w, pool_b, pool_scale, dn_w_in, dn_conv_w, dn_a_log, dn_dt_bias, dn_norm_g, dn_w_out):
    row, col = grid_positions(x.shape[1])
    cond = jax.nn.silu(c)[:, None, :]
    cond_ctx = jax.nn.silu(c_ctx)[None, None, :]
    for i in range(DEPTH):
        kind, inst = i % N_MIXERS, i // N_MIXERS
        need_ctx = i < DEPTH - 1
        m_lat = jnp.split(cond @ ada_w[i] + ada_b[i], N_MOD, axis=-1)
        m_ctx = jnp.split(cond_ctx @ ada_w[i] + ada_b[i], N_MOD, axis=-1)
        u_lat = rmsnorm(x, mix_pre_g[i]) * (1 + m_lat[1]) + m_lat[0]
        u_ctx = rmsnorm(ctx, mix_pre_g[i]) * (1 + m_ctx[1]) + m_ctx[0]
        if kind == 0:
            y_lat, y_ctx = retention_mixer(u_lat, u_ctx, ret_w_in[inst], ret_decay_logit[inst], ret_w_out[inst],
                                           row, col, need_ctx)
        elif kind == 1:
            y_lat, y_ctx = window_attention_mixer(u_lat, u_ctx, att_w_in[inst], att_sink[inst], att_w_out[inst],
                                                  row, col, need_ctx)
        elif kind == 2:
            y_lat, y_ctx = pool_mixer(u_lat, u_ctx, pool_w[inst], pool_b[inst], pool_scale[inst], need_ctx)
        else:
            y_lat, y_ctx = deltanet_mixer(u_lat, u_ctx, dn_w_in[inst], dn_conv_w[inst], dn_a_log[inst],
                                          dn_dt_bias[inst], dn_norm_g[inst], dn_w_out[inst], need_ctx)
        x = x + m_lat[2] * rmsnorm(y_lat, mix_post_g[i])
        x = ffn_sublayer(x, m_lat[3], m_lat[4], m_lat[5], mlp_pre_g[i], mlp_post_g[i], mlp_w1[i], mlp_w2[i])
        if need_ctx:
            ctx = ctx + m_ctx[2] * rmsnorm(y_ctx, mix_post_g[i])
            ctx = ffn_sublayer(ctx, m_ctx[3], m_ctx[4], m_ctx[5], mlp_pre_g[i], mlp_post_g[i], mlp_w1[i], mlp_w2[i])
    return x
```

```python
import functools

import numpy as np
import jax
import jax.numpy as jnp
from jax import lax
from jax.experimental import pallas as pl
from jax.experimental.pallas import tpu as pltpu

F32 = jnp.float32
BF16 = jnp.bfloat16
HI = lax.Precision.HIGHEST

EPS = 1e-6
NEG_INF = -1e30
ROPE_BASE = 10000.0
GRID_W = 64
N_MOD = 6
RET_HEADS = 8
RET_CHUNK = 128
ATT_HEADS = 16
ATT_KV_HEADS = 4
ATT_WINDOW = 128
ATT_BLOCK = 128
POOL_WINDOWS = (2, 4, 8, 16)
POOL_PAD = 16
DN_HEADS = 8
DN_CHUNK = 64
DN_CONV_W = 5

LANES = 128
VMEM_LIMIT = 48 * 1024 * 1024


def _cparams(sem):
    return pltpu.CompilerParams(dimension_semantics=sem, vmem_limit_bytes=VMEM_LIMIT)


def _dot(a, b):
    return jnp.dot(a, b, preferred_element_type=F32)


def _dot_nt(a, b):
    return lax.dot_general(a, b, (((1,), (1,)), ((), ())), preferred_element_type=F32)


def _dot_hi(a, b):
    return jnp.dot(a, b, preferred_element_type=F32, precision=HI)


def _rms(x):
    return x * lax.rsqrt(jnp.mean(x * x, axis=-1, keepdims=True) + EPS)


def _silu(x):
    return x * jax.nn.sigmoid(x)


def _pick_tile(n, pref):
    t = min(pref, n)
    while n % t:
        t //= 2
    return t


def _mod_kernel(c_ref, w_ref, b_ref, o_ref):
    o_ref[0] = _dot_hi(_silu(c_ref[...]), w_ref[0]) + b_ref[0]


def _modulation(cond, ada_w, ada_b):
    depth, d, n = ada_w.shape
    rows = cond.shape[0]
    tn = _pick_tile(n, 1536)
    return pl.pallas_call(
        _mod_kernel,
        grid=(depth, n // tn),
        in_specs=[pl.BlockSpec((rows, d), lambda i, j: (0, 0)),
                  pl.BlockSpec((1, d, tn), lambda i, j: (i, 0, j)),
                  pl.BlockSpec((1, 1, tn), lambda i, j: (i, 0, j))],
        out_specs=pl.BlockSpec((1, rows, tn), lambda i, j: (i, 0, j)),
        out_shape=jax.ShapeDtypeStruct((depth, rows, n), F32),
        compiler_params=_cparams(("parallel", "parallel")),
        name="modulation",
    )(cond, ada_w, ada_b.reshape(depth, 1, n))


def _inproj_kernel(x_ref, mod_ref, g_ref, w_ref, *rest, has_aux):
    if has_aux:
        wa_ref, o_ref, oa_ref, u_scr = rest
    else:
        o_ref, u_scr = rest

    @pl.when(pl.program_id(2) == 0)
    def _():
        m = mod_ref[0]
        u = _rms(x_ref[0]) * g_ref[...] * (1.0 + m[1:2]) + m[0:1]
        u_scr[...] = u.astype(BF16)
        if has_aux:
            oa_ref[0] = _dot(u_scr[...], wa_ref[...])

    o_ref[0] = _dot(u_scr[...], w_ref[...]).astype(o_ref.dtype)


def _inproj(x, mod, g, w, aux_w=None, name="inproj"):
    bx, lx, d = x.shape
    n = w.shape[1]
    tm = _pick_tile(lx, 1024)
    tn = _pick_tile(n, 512)
    has_aux = aux_w is not None
    in_specs = [pl.BlockSpec((1, tm, d), lambda b, i, j: (b, i, 0)),
                pl.BlockSpec((1, N_MOD, d), lambda b, i, j: (b, 0, 0)),
                pl.BlockSpec((1, d), lambda b, i, j: (0, 0)),
                pl.BlockSpec((d, tn), lambda b, i, j: (0, j))]
    out_specs = [pl.BlockSpec((1, tm, tn), lambda b, i, j: (b, i, j))]
    out_shape = [jax.ShapeDtypeStruct((bx, lx, n), BF16)]
    args = [x, mod, g.reshape(1, d), w]
    if has_aux:
        na = aux_w.shape[1]
        in_specs.append(pl.BlockSpec((d, na), lambda b, i, j: (0, 0)))
        out_specs.append(pl.BlockSpec((1, tm, na), lambda b, i, j: (b, i, 0)))
        out_shape.append(jax.ShapeDtypeStruct((bx, lx, na), F32))
        args.append(aux_w)
    outs = pl.pallas_call(
        functools.partial(_inproj_kernel, has_aux=has_aux),
        grid=(bx, lx // tm, n // tn),
        in_specs=in_specs, out_specs=out_specs, out_shape=out_shape,
        scratch_shapes=[pltpu.VMEM((tm, d), BF16)],
        compiler_params=_cparams(("parallel", "parallel", "arbitrary")),
        name=name,
    )(*args)
    return outs if has_aux else outs[0]


def _post_kernel(a_ref, x_ref, mod_ref, gpost_ref, gpre2_ref, gpost2_ref, wo_ref, *rest, grouped):
    if grouped:
        pb_ref, ps_ref, w1_ref, w2_ref, o_ref, x1_scr, u_scr, acc_scr = rest
    else:
        w1_ref, w2_ref, o_ref, x1_scr, u_scr, acc_scr = rest
    j = pl.program_id(2)

    @pl.when(j == 0)
    def _():
        a = a_ref[0]
        if grouped:
            ng, gw, _ = wo_ref.shape
            y = jnp.concatenate([_dot(a[:, g * gw:(g + 1) * gw], wo_ref[g]) for g in range(ng)], axis=1)
            y = (y + pb_ref[...]) * ps_ref[...]
        else:
            y = _dot(a, wo_ref[...])
        m = mod_ref[0]
        x1 = x_ref[0] + m[2:3] * (_rms(y) * gpost_ref[...])
        x1_scr[...] = x1
        u = _rms(x1) * gpre2_ref[...] * (1.0 + m[4:5]) + m[3:4]
        u_scr[...] = u.astype(BF16)
        acc_scr[...] = jnp.zeros_like(acc_scr)

    h = jnp.square(jnp.maximum(_dot(u_scr[...], w1_ref[...]), 0.0)).astype(BF16)
    acc_scr[...] += _dot(h, w2_ref[...])

    @pl.when(j == pl.num_programs(2) - 1)
    def _():
        m = mod_ref[0]
        o_ref[0] = x1_scr[...] + m[5:6] * (_rms(acc_scr[...]) * gpost2_ref[...])


def _post(a, x, mod, gpost, gpre2, gpost2, w_out, w1, w2, pool_bias=None, pool_scale=None, name="post"):
    bx, lx, d = x.shape
    ka = a.shape[2]
    dff = w1.shape[1]
    tm = _pick_tile(lx, 512)
    tf = _pick_tile(dff, 512)
    grouped = pool_bias is not None
    row = lambda v: v.reshape(1, d)
    const2 = lambda b, i, j: (0, 0)
    in_specs = [pl.BlockSpec((1, tm, ka), lambda b, i, j: (b, i, 0)),
                pl.BlockSpec((1, tm, d), lambda b, i, j: (b, i, 0)),
                pl.BlockSpec((1, N_MOD, d), lambda b, i, j: (b, 0, 0)),
                pl.BlockSpec((1, d), const2), pl.BlockSpec((1, d), const2), pl.BlockSpec((1, d), const2)]
    args = [a, x, mod, row(gpost), row(gpre2), row(gpost2), w_out]
    if grouped:
        in_specs += [pl.BlockSpec(w_out.shape, lambda b, i, j: (0, 0, 0)),
                     pl.BlockSpec((1, d), const2), pl.BlockSpec((1, d), const2)]
        args += [row(pool_bias), row(pool_scale)]
    else:
        in_specs += [pl.BlockSpec(w_out.shape, const2)]
    in_specs += [pl.BlockSpec((d, tf), lambda b, i, j: (0, j)), pl.BlockSpec((tf, d), lambda b, i, j: (j, 0))]
    args += [w1, w2]
    return pl.pallas_call(
        functools.partial(_post_kernel, grouped=grouped),
        grid=(bx, lx // tm, dff // tf),
        in_specs=in_specs,
        out_specs=pl.BlockSpec((1, tm, d), lambda b, i, j: (b, i, 0)),
        out_shape=jax.ShapeDtypeStruct((bx, lx, d), F32),
        scratch_shapes=[pltpu.VMEM((tm, d), F32), pltpu.VMEM((tm, d), BF16), pltpu.VMEM((tm, d), F32)],
        compiler_params=_cparams(("parallel", "parallel", "arbitrary")),
        name=name,
    )(*args)


def _rope_tables(n, head_dim, n_rep, scale=1.0):
    quarter = head_dim // 4
    inv = ROPE_BASE ** (-jnp.arange(quarter, dtype=F32) / quarter)
    t = jnp.arange(n, dtype=jnp.int32)
    ang_r = (t // GRID_W).astype(F32)[:, None] * inv[None, :]
    ang_c = (t % GRID_W).astype(F32)[:, None] * inv[None, :]
    cos = jnp.concatenate([jnp.cos(ang_r), jnp.cos(ang_r), jnp.cos(ang_c), jnp.cos(ang_c)], axis=1)
    sin = jnp.concatenate([-jnp.sin(ang_r), jnp.sin(ang_r), -jnp.sin(ang_c), jnp.sin(ang_c)], axis=1)
    return jnp.tile(cos, (1, n_rep)) * scale, jnp.tile(sin, (1, n_rep)) * scale


def _rope(x, cos, sin, quarter):
    w = x.shape[1]
    lane = lax.broadcasted_iota(jnp.int32, x.shape, 1)
    first = (lane & (2 * quarter - 1)) < quarter
    partner = jnp.where(first, pltpu.roll(x, w - quarter, 1), pltpu.roll(x, quarter, 1))
    return x * cos + partner * sin


def _head_scalar(vec, h):
    lane = lax.broadcasted_iota(jnp.int32, vec.shape, 1)
    return jnp.sum(jnp.where(lane == h, vec, 0.0), axis=1, keepdims=True)


def _log_sigmoid(x):
    return jnp.minimum(x, 0.0) - jnp.log1p(jnp.exp(-jnp.abs(x)))


def _run_bidirectional(n, step, want_out):
    if want_out:
        assert n % 2 == 0
        lax.fori_loop(0, n // 2, lambda i, c: step(i, n - 1 - i, False) or c, 0)
        lax.fori_loop(n // 2, n, lambda i, c: step(i, n - 1 - i, True) or c, 0)
    else:
        lax.fori_loop(0, n, lambda i, c: step(i, n - 1 - i, None) or c, 0)


def _ret_kernel(dl_ref, cos_ref, sin_ref, ql_ref, kl_ref, vl_ref, gfl_ref, gbl_ref,
                qc_ref, kc_ref, vc_ref, gfc_ref, gbc_ref, yl_ref, yc_ref,
                q_scr, k_scr, sf_scr, sb_scr, yacc_scr):
    c = RET_CHUNK
    l = ql_ref.shape[1]
    lc = qc_ref.shape[1]
    dk = ql_ref.shape[2]
    h = pl.program_id(1)
    scale = dk ** -0.5

    ls = _log_sigmoid(dl_ref[...])
    lgf = _head_scalar(ls[0:1], h)
    lgb = _head_scalar(ls[1:2], h)

    q_scr[0:lc, :] = (qc_ref[0].astype(F32) * scale).astype(BF16)
    k_scr[0:lc, :] = kc_ref[0]
    rb = _pick_tile(l, 256)

    def rope_body(r, carry):
        rows = pl.ds(pl.multiple_of(r * rb, rb), rb)
        dst = pl.ds(pl.multiple_of(lc + r * rb, rb), rb)
        cs, sn = cos_ref[rows, :], sin_ref[rows, :]
        q_scr[dst, :] = _rope(ql_ref[0, rows, :].astype(F32) * scale, cs, sn, dk // 4).astype(BF16)
        k_scr[dst, :] = _rope(kl_ref[0, rows, :].astype(F32), cs, sn, dk // 4).astype(BF16)
        return carry

    lax.fori_loop(0, l // rb, rope_body, 0)

    ii = lax.broadcasted_iota(jnp.int32, (c, c), 0).astype(F32)
    jj = lax.broadcasted_iota(jnp.int32, (c, c), 1).astype(F32)
    diff = ii - jj
    dec_f = jnp.where(diff >= 0, jnp.exp(lgf * jnp.maximum(diff, 0.0)), 0.0)
    dec_b = jnp.where(diff <= 0, jnp.exp(lgb * jnp.maximum(-diff, 0.0)), 0.0)
    idx = lax.broadcasted_iota(jnp.int32, (c, 1), 0).astype(F32)
    qdec_f, kdec_f, cd_f = jnp.exp(lgf * (idx + 1.0)), jnp.exp(lgf * (c - 1.0 - idx)), jnp.exp(lgf * c)
    qdec_b, kdec_b, cd_b = jnp.exp(lgb * (c - idx)), jnp.exp(lgb * idx), jnp.exp(lgb * c)

    sf_scr[...] = jnp.zeros_like(sf_scr)
    sb_scr[...] = jnp.zeros_like(sb_scr)

    def one_dir(base, ci, v_ref, g_ref, s_scr, dec, qdec, kdec, cd):
        rows = pl.ds(pl.multiple_of(ci * c, c), c)
        srows = pl.ds(pl.multiple_of(base + ci * c, c), c)
        q, k, v = q_scr[srows, :], k_scr[srows, :], v_ref[0, rows, :]
        s = s_scr[...]
        scores = (_dot_nt(q, k) * dec).astype(BF16)
        o = _dot(scores, v) + qdec * _dot(q, s.astype(BF16))
        kd = (k.astype(F32) * kdec).T.astype(BF16)
        s_scr[...] = s * cd + _dot(kd, v)
        y = o * lax.rsqrt(jnp.mean(o * o, axis=-1, keepdims=True) + EPS)
        return rows, y * _silu(g_ref[0, rows, :].astype(F32))

    def make_step(base, v_ref, gf_ref, gb_ref, y_ref):
        def step(cf, cb, second_half):
            rows_f, yf = one_dir(base, cf, v_ref, gf_ref, sf_scr, dec_f, qdec_f, kdec_f, cd_f)
            rows_b, yb = one_dir(base, cb, v_ref, gb_ref, sb_scr, dec_b, qdec_b, kdec_b, cd_b)
            if second_half:
                y_ref[0, rows_f, :] = (yacc_scr[rows_f, :] + yf).astype(y_ref.dtype)
                y_ref[0, rows_b, :] = (yacc_scr[rows_b, :] + yb).astype(y_ref.dtype)
            else:
                yacc_scr[rows_f, :] = yf
                yacc_scr[rows_b, :] = yb
        return step

    _run_bidirectional(lc // c, make_step(0, vc_ref, gfc_ref, gbc_ref, yc_ref), True)
    _run_bidirectional(l // c, make_step(lc, vl_ref, gfl_ref, gbl_ref, yl_ref), True)


def _retention_core(p_lat, p_ctx, decay_logit, cos, sin):
    b, l, _ = p_lat.shape
    lc = p_ctx.shape[1]
    nh = RET_HEADS
    dk = cos.shape[1]
    dv = 2 * dk

    def specs(n):
        return [pl.BlockSpec((1, n, dk), lambda bi, h: (bi, 0, h)),
                pl.BlockSpec((1, n, dk), lambda bi, h: (bi, 0, nh + h)),
                pl.BlockSpec((1, n, dv), lambda bi, h: (bi, 0, nh + h)),
                pl.BlockSpec((1, n, dv), lambda bi, h: (bi, 0, 2 * nh + h)),
                pl.BlockSpec((1, n, dv), lambda bi, h: (bi, 0, 3 * nh + h))]

    const2 = lambda bi, h: (0, 0)
    return pl.pallas_call(
        _ret_kernel,
        grid=(b, nh),
        in_specs=[pl.BlockSpec(decay_logit.shape, const2), pl.BlockSpec((l, dk), const2),
                  pl.BlockSpec((l, dk), const2)] + specs(l) + specs(lc),
        out_specs=[pl.BlockSpec((1, l, dv), lambda bi, h: (bi, 0, h)),
                   pl.BlockSpec((1, lc, dv), lambda bi, h: (bi, 0, h))],
        out_shape=[jax.ShapeDtypeStruct((b, l, nh * dv), BF16), jax.ShapeDtypeStruct((b, lc, nh * dv), BF16)],
        scratch_shapes=[pltpu.VMEM((lc + l, dk), BF16), pltpu.VMEM((lc + l, dk), BF16),
                        pltpu.VMEM((dk, dv), F32), pltpu.VMEM((dk, dv), F32),
                        pltpu.VMEM((max(l, lc), dv), F32)],
        compiler_params=_cparams(("parallel", "parallel")),
        name="retention_core",
    )(decay_logit, cos, sin, *([p_lat] * 5), *([p_ctx] * 5))


def _sink_softmax_out(parts, sink_col):
    mx = sink_col
    for s, _ in parts:
        mx = jnp.maximum(mx, jnp.max(s, axis=-1, keepdims=True))
    den = jnp.exp(sink_col - mx)
    acc = None
    for s, v in parts:
        p = jnp.exp(s - mx)
        den = den + jnp.sum(p, axis=-1, keepdims=True)
        pv = _dot(p.astype(BF16), v)
        acc = pv if acc is None else acc + pv
    return acc / den


def _att_kernel(sink_ref, cos_ref, sin_ref, pl_ref, pc_ref, ol_ref, *rest, with_ctx_out):
    if with_ctx_out:
        oc_ref, r_scr = rest
    else:
        (r_scr,) = rest
    grp = ATT_HEADS // ATT_KV_HEADS
    qb, win = ATT_BLOCK, ATT_WINDOW
    span = qb + 2 * win
    l = pl_ref.shape[1]
    lc = pc_ref.shape[1]
    dh = pl_ref.shape[2] // (grp + 2)
    nq = grp * dh
    kh = pl.program_id(1)

    rb = _pick_tile(l, 256)

    def rope_body(r, carry):
        rows = pl.ds(pl.multiple_of(r * rb, rb), rb)
        x = pl_ref[0, rows, :].astype(F32)
        r_scr[rows, :] = _rope(x, cos_ref[rows, :], sin_ref[rows, :], dh // 4).astype(BF16)
        return carry

    lax.fori_loop(0, l // rb, rope_body, 0)

    def stack_heads(q):
        return jnp.concatenate([q[:, g * dh:(g + 1) * dh] for g in range(grp)], axis=0)

    def unstack_heads(o, m):
        return jnp.concatenate([o[g * m:(g + 1) * m, :] for g in range(grp)], axis=1)

    def sink_col(m):
        return jnp.concatenate([jnp.full((m, 1), sink_ref[kh * grp + g], F32) for g in range(grp)], axis=0)

    k_ctx = pc_ref[0, :, nq:nq + dh]
    v_ctx = pc_ref[0, :, nq + dh:nq + 2 * dh]
    sink_blk = sink_col(qb)
    qi = lax.broadcasted_iota(jnp.int32, (grp * qb, span), 0) & (qb - 1)
    kj = lax.broadcasted_iota(jnp.int32, (grp * qb, span), 1)

    def block(bi, carry):
        qs = pl.multiple_of(bi * qb, qb)
        ks = pl.multiple_of(jnp.clip(qs - win, 0, l - span), qb)
        q = stack_heads(r_scr[pl.ds(qs, qb), 0:nq])
        kv = r_scr[pl.ds(ks, span), nq:nq + 2 * dh]
        valid = jnp.abs((ks + kj) - (qs + qi)) <= win
        s_loc = jnp.where(valid, _dot_nt(q, kv[:, 0:dh]), NEG_INF)
        s_ctx = _dot_nt(q, k_ctx)
        o = _sink_softmax_out([(s_loc, kv[:, dh:2 * dh]), (s_ctx, v_ctx)], sink_blk)
        ol_ref[0, pl.ds(qs, qb), :] = unstack_heads(o, qb).astype(ol_ref.dtype)
        return carry

    lax.fori_loop(0, l // qb, block, 0)

    if with_ctx_out:
        qc = stack_heads((pc_ref[0, :, 0:nq].astype(F32) * dh ** -0.5).astype(BF16))
        o = _sink_softmax_out([(_dot_nt(qc, k_ctx), v_ctx)], sink_col(lc))
        oc_ref[0] = unstack_heads(o, lc).astype(oc_ref.dtype)


def _attention_core(p_lat, p_ctx, sink, cos, sin, with_ctx_out):
    b, l, n = p_lat.shape
    lc = p_ctx.shape[1]
    nkv = ATT_KV_HEADS
    wblk = n // nkv
    nq = wblk * (ATT_HEADS // nkv) // (ATT_HEADS // nkv + 2)
    const2 = lambda bi, h: (0, 0)
    out_specs = [pl.BlockSpec((1, l, nq), lambda bi, h: (bi, 0, h))]
    out_shape = [jax.ShapeDtypeStruct((b, l, nkv * nq), BF16)]
    if with_ctx_out:
        out_specs.append(pl.BlockSpec((1, lc, nq), lambda bi, h: (bi, 0, h)))
        out_shape.append(jax.ShapeDtypeStruct((b, lc, nkv * nq), BF16))
    outs = pl.pallas_call(
        functools.partial(_att_kernel, with_ctx_out=with_ctx_out),
        grid=(b, nkv),
        in_specs=[pl.BlockSpec(memory_space=pltpu.SMEM),
                  pl.BlockSpec((l, wblk), const2), pl.BlockSpec((l, wblk), const2),
                  pl.BlockSpec((1, l, wblk), lambda bi, h: (bi, 0, h)),
                  pl.BlockSpec((1, lc, wblk), lambda bi, h: (bi, 0, h))],
        out_specs=out_specs, out_shape=out_shape,
        scratch_shapes=[pltpu.VMEM((l, wblk), BF16)],
        compiler_params=_cparams(("parallel", "parallel")),
        name="attention_core",
    )(sink, cos, sin, p_lat, p_ctx)
    return (outs[0], outs[1]) if with_ctx_out else (outs[0], None)


def _pool_kernel(x_ref, mod_ref, g_ref, o_ref, rinv_scr, up_scr):
    n, d = x_ref.shape[1], x_ref.shape[2]
    ng = len(POOL_WINDOWS)
    gw = d // ng
    pad = POOL_PAD
    m = mod_ref[0]
    x = x_ref[0]
    rinv_scr[...] = lax.rsqrt(jnp.mean(x * x, axis=-1, keepdims=True) + EPS)
    up_scr[0:pad, :] = jnp.zeros((pad, gw), F32)
    up_scr[pad + n:pad + n + pad, :] = jnp.zeros((pad, gw), F32)
    t = lax.broadcasted_iota(jnp.int32, (n, 1), 0)
    for gi, w in enumerate(POOL_WINDOWS):
        cols = slice(gi * gw, (gi + 1) * gw)
        u = x_ref[0, :, cols] * rinv_scr[...] * g_ref[:, cols] * (1.0 + m[1:2, cols]) + m[0:1, cols]
        up_scr[pad:pad + n, :] = u
        lo, hi = w // 2, w - 1 - w // 2
        tot = up_scr[pad - lo:pad - lo + n, :]
        for dlt in range(-lo + 1, hi + 1):
            tot = tot + up_scr[pad + dlt:pad + dlt + n, :]
        cnt = (jnp.minimum(t + hi + 1, n) - jnp.maximum(t - lo, 0)).astype(F32)
        o_ref[0, :, cols] = (tot / cnt - u).astype(o_ref.dtype)


def _pool_core(x, mod, g):
    bx, n, d = x.shape
    gw = d // len(POOL_WINDOWS)
    return pl.pallas_call(
        _pool_kernel,
        grid=(bx,),
        in_specs=[pl.BlockSpec((1, n, d), lambda b: (b, 0, 0)),
                  pl.BlockSpec((1, N_MOD, d), lambda b: (b, 0, 0)),
                  pl.BlockSpec((1, d), lambda b: (0, 0))],
        out_specs=pl.BlockSpec((1, n, d), lambda b: (b, 0, 0)),
        out_shape=jax.ShapeDtypeStruct((bx, n, d), BF16),
        scratch_shapes=[pltpu.VMEM((n, 1), F32), pltpu.VMEM((n + 2 * POOL_PAD, gw), F32)],
        compiler_params=_cparams(("parallel",)),
        name="pool_core",
    )(x, mod, g.reshape(1, d))


def _unit_tri_inverse(a):
    c = a.shape[0]
    ii = lax.broadcasted_iota(jnp.int32, (c, c), 0)
    jj = lax.broadcasted_iota(jnp.int32, (c, c), 1)
    blk = lambda t, k: lax.shift_right_logical(t, k)
    d = (ii == jj).astype(F32) - jnp.where((blk(ii, 1) == blk(jj, 1)) & (ii != jj), a, 0.0)
    k = 1
    while (2 << k) <= c:
        off = jnp.where((blk(ii, k + 1) == blk(jj, k + 1)) & (blk(ii, k) != blk(jj, k)), a, 0.0)
        d = d - _dot_hi(d, _dot_hi(off, d))
        k += 1
    return d


def _dn_kernel(alog_ref, dtb_ref, ng_ref, cw_ref_q, cw_ref_k, cw_ref_v,
               ql_ref, kl_ref, vl_ref, gfl_ref, gbl_ref, gl_ref,
               qc_ref, kc_ref, vc_ref, gfc_ref, gbc_ref, gc_ref,
               yl_ref, *rest, with_ctx_out):
    if with_ctx_out:
        yc_ref, q_scr, k_scr, v_scr, xp_scr, sf_scr, sb_scr, yacc_scr = rest
    else:
        yc_ref = None
        q_scr, k_scr, v_scr, xp_scr, sf_scr, sb_scr, yacc_scr = rest
    c = DN_CHUNK
    nh = DN_HEADS
    l = ql_ref.shape[1]
    lc = qc_ref.shape[1]
    dk = ql_ref.shape[2]
    h = pl.program_id(1)
    kw = DN_CONV_W
    cpad = 8

    def conv_seq(src_ref, cw_ref, dst_scr, base, n, norm, scale):
        xp_scr[0:cpad, :] = jnp.zeros((cpad, dk), F32)
        xp_scr[cpad:cpad + n, :] = src_ref[0].astype(F32)
        xp_scr[cpad + n:cpad + n + cpad, :] = jnp.zeros((cpad, dk), F32)
        rb = _pick_tile(n, 256)

        def body(r, carry):
            start = pl.multiple_of(r * rb, rb)
            win = xp_scr[pl.ds(start, rb + 2 * cpad), :]
            y = None
            for t in range(kw):
                o = cpad - kw // 2 + t
                term = win[o:o + rb, :] * cw_ref[t:t + 1, :]
                y = term if y is None else y + term
            y = _silu(y)
            if norm:
                y = y * lax.rsqrt(jnp.sum(y * y, axis=-1, keepdims=True) + EPS) * scale
            dst_scr[pl.ds(pl.multiple_of(base + start, rb), rb), :] = y.astype(BF16)
            return carry

        lax.fori_loop(0, n // rb, body, 0)

    for src_c, src_l, cw, dst, norm, scale in ((qc_ref, ql_ref, cw_ref_q, q_scr, True, dk ** -0.5),
                                               (kc_ref, kl_ref, cw_ref_k, k_scr, True, 1.0),
                                               (vc_ref, vl_ref, cw_ref_v, v_scr, False, 1.0)):
        conv_seq(src_c, cw, dst, 0, lc, norm, scale)
        conv_seq(src_l, cw, dst, lc, l, norm, scale)

    lane = lax.broadcasted_iota(jnp.int32, (1, LANES), 1)
    neg_a = -jnp.exp(alog_ref[...])
    dtb = dtb_ref[...]
    ii = lax.broadcasted_iota(jnp.int32, (c, c), 0)
    jj = lax.broadcasted_iota(jnp.int32, (c, c), 1)
    tri_l = (ii >= jj).astype(F32)
    tri_u = (ii <= jj).astype(F32)
    norm_g = ng_ref[...]

    sf_scr[...] = jnp.zeros_like(sf_scr)
    sb_scr[...] = jnp.zeros_like(sb_scr)

    def lane_col(x, idx):
        return jnp.broadcast_to(jnp.sum(jnp.where(lane == idx, x, 0.0), axis=1, keepdims=True), (c, LANES))

    def one_dir(base, ci, g_ref, gate_ref, s_scr, upper):
        rows = pl.ds(pl.multiple_of(ci * c, c), c)
        srows = pl.ds(pl.multiple_of(base + ci * c, c), c)
        q, k, v = q_scr[srows, :], k_scr[srows, :], v_scr[srows, :]
        graw = g_ref[0, rows, :]
        beta = lane_col(jax.nn.sigmoid(graw), h + (nh if upper else 0))
        la = lane_col(neg_a * jax.nn.softplus(graw + dtb), h + (3 * nh if upper else 2 * nh))
        g = _dot_hi(tri_u if upper else tri_l, la)
        g_row = g.T[0:c, 0:c]
        g_col = g[:, 0:c]
        keep = (ii <= jj) if upper else (ii >= jj)
        decay = jnp.where(keep, jnp.exp(jnp.where(keep, g_col - g_row, 0.0)), 0.0)
        kf = k.astype(F32)
        kk = _dot_nt(k, k)
        a = jnp.where(ii != jj, kk * beta[:, 0:c] * decay, 0.0)
        t_inv = _unit_tri_inverse(a)
        eg = jnp.exp(g)
        rhs = jnp.concatenate([v.astype(F32) * beta, kf * (beta * eg)], axis=1).astype(BF16)
        sol = _dot(t_inv.astype(BF16), rhs)
        u, w = sol[:, 0:dk], sol[:, dk:2 * dk]
        attn = (_dot_nt(q, k) * decay).astype(BF16)
        g_last = g[0:1, :] if upper else g[c - 1:c, :]
        s = s_scr[...]
        sb = s.astype(BF16)
        v_new = u - _dot(w.astype(BF16), sb)
        o = eg * _dot(q, sb) + _dot(attn, v_new.astype(BF16))
        kg = (kf * jnp.exp(g_last - g)).T.astype(BF16)
        s_scr[...] = s * jnp.exp(g_last) + _dot(kg, v_new.astype(BF16))
        if gate_ref is None:
            return rows, None
        y = o * lax.rsqrt(jnp.mean(o * o, axis=-1, keepdims=True) + EPS) * norm_g
        return rows, y * _silu(gate_ref[0, rows, :].astype(F32))

    def make_step(base, g_ref, gf_ref, gb_ref, y_ref):
        def step(cf, cb, second_half):
            rows_f, yf = one_dir(base, cf, g_ref, gf_ref, sf_scr, False)
            rows_b, yb = one_dir(base, cb, g_ref, gb_ref, sb_scr, True)
            if second_half is None:
                return
            if second_half:
                y_ref[0, rows_f, :] = (yacc_scr[rows_f, :] + yf).astype(y_ref.dtype)
                y_ref[0, rows_b, :] = (yacc_scr[rows_b, :] + yb).astype(y_ref.dtype)
            else:
                yacc_scr[rows_f, :] = yf
                yacc_scr[rows_b, :] = yb
        return step

    if with_ctx_out:
        _run_bidirectional(lc // c, make_step(0, gc_ref, gfc_ref, gbc_ref, yc_ref), True)
    else:
        _run_bidirectional(lc // c, make_step(0, gc_ref, None, None, None), False)
    _run_bidirectional(l // c, make_step(lc, gl_ref, gfl_ref, gbl_ref, yl_ref), True)


def _deltanet_core(p_lat, g_lat, p_ctx, g_ctx, conv_w, a_lanes, dtb_lanes, norm_g, with_ctx_out):
    b, l, _ = p_lat.shape
    lc = p_ctx.shape[1]
    nh = DN_HEADS
    dk = norm_g.shape[-1]

    def specs(n):
        sp = [pl.BlockSpec((1, n, dk), lambda bi, h, k=k: (bi, 0, k * nh + h)) for k in range(5)]
        return sp + [pl.BlockSpec((1, n, LANES), lambda bi, h: (bi, 0, 0))]

    const2 = lambda bi, h: (0, 0)
    kw = conv_w.shape[0]
    out_specs = [pl.BlockSpec((1, l, dk), lambda bi, h: (bi, 0, h))]
    out_shape = [jax.ShapeDtypeStruct((b, l, nh * dk), BF16)]
    if with_ctx_out:
        out_specs.append(pl.BlockSpec((1, lc, dk), lambda bi, h: (bi, 0, h)))
        out_shape.append(jax.ShapeDtypeStruct((b, lc, nh * dk), BF16))
    outs = pl.pallas_call(
        functools.partial(_dn_kernel, with_ctx_out=with_ctx_out),
        grid=(b, nh),
        in_specs=[pl.BlockSpec((1, LANES), const2), pl.BlockSpec((1, LANES), const2), pl.BlockSpec((1, dk), const2)]
                 + [pl.BlockSpec((kw, dk), lambda bi, h, k=k: (0, k * nh + h)) for k in range(3)]
                 + specs(l) + specs(lc),
        out_specs=out_specs, out_shape=out_shape,
        scratch_shapes=[pltpu.VMEM((lc + l, dk), BF16), pltpu.VMEM((lc + l, dk), BF16),
                        pltpu.VMEM((lc + l, dk), BF16), pltpu.VMEM((max(l, lc) + 16, dk), F32),
                        pltpu.VMEM((dk, dk), F32), pltpu.VMEM((dk, dk), F32),
                        pltpu.VMEM((max(l, lc), dk), F32)],
        compiler_params=_cparams(("parallel", "parallel")),
        name="deltanet_core",
    )(a_lanes, dtb_lanes, norm_g.reshape(1, dk), conv_w, conv_w, conv_w,
      *([p_lat] * 5), g_lat, *([p_ctx] * 5), g_ctx)
    return (outs[0], outs[1]) if with_ctx_out else (outs[0], None)


def kernel(x, c, ctx, c_ctx, ada_w, ada_b, mix_pre_g, mix_post_g, mlp_pre_g, mlp_post_g, mlp_w1, mlp_w2, ret_w_in, ret_decay_logit, ret_w_out, att_w_in, att_sink, att_w_out, pool_w, pool_b, pool_scale, dn_w_in, dn_conv_w, dn_a_log, dn_dt_bias, dn_norm_g, dn_w_out):
    b, l, d = x.shape
    lc = ctx.shape[1]
    depth = ada_w.shape[0]
    n_mixers = 4

    rows = -(-(b + 1) // 8) * 8
    cond = jnp.zeros((rows, d), F32).at[:b].set(c).at[b].set(c_ctx)
    mods = _modulation(cond, ada_w, ada_b)
    xc = ctx.reshape(1, b * lc, d)

    for i in range(depth):
        kind, inst = i % n_mixers, i // n_mixers
        need_ctx = i < depth - 1
        m_lat = mods[i, :b].reshape(b, N_MOD, d)
        m_ctx = mods[i, b:b + 1].reshape(1, N_MOD, d)
        w1, w2 = mlp_w1[i].astype(BF16), mlp_w2[i].astype(BF16)
        post_kw = {}
        if kind == 0:
            w_in = ret_w_in[inst].astype(BF16)
            p_lat = _inproj(x, m_lat, mix_pre_g[i], w_in, name="ret_inproj")
            p_ctx = _inproj(xc, m_ctx, mix_pre_g[i], w_in, name="ret_inproj_ctx").reshape(b, lc, -1)
            dk = d // RET_HEADS
            cos, sin = _rope_tables(l, dk, 1)
            a_lat, a_ctx = _retention_core(p_lat, p_ctx, ret_decay_logit[inst], cos, sin)
            w_out = ret_w_out[inst].astype(BF16)
        elif kind == 1:
            grp = ATT_HEADS // ATT_KV_HEADS
            dh = d // ATT_HEADS
            nq, nkv = ATT_HEADS * dh, ATT_KV_HEADS * dh
            wq = att_w_in[inst][:, :nq].reshape(d, ATT_KV_HEADS, grp * dh)
            wk = att_w_in[inst][:, nq:nq + nkv].reshape(d, ATT_KV_HEADS, dh)
            wv = att_w_in[inst][:, nq + nkv:].reshape(d, ATT_KV_HEADS, dh)
            w_in = jnp.concatenate([wq, wk, wv], axis=2).reshape(d, nq + 2 * nkv).astype(BF16)
            p_lat = _inproj(x, m_lat, mix_pre_g[i], w_in, name="att_inproj")
            p_ctx = _inproj(xc, m_ctx, mix_pre_g[i], w_in, name="att_inproj_ctx").reshape(b, lc, -1)
            cos_h, sin_h = _rope_tables(l, dh, 1)
            one, zero = jnp.ones((l, dh), F32), jnp.zeros((l, dh), F32)
            qs = dh ** -0.5
            cos = jnp.concatenate([jnp.tile(cos_h, (1, grp)) * qs, cos_h, one], axis=1)
            sin = jnp.concatenate([jnp.tile(sin_h, (1, grp)) * qs, sin_h, zero], axis=1)
            a_lat, a_ctx = _attention_core(p_lat, p_ctx, att_sink[inst], cos, sin, need_ctx)
            w_out = att_w_out[inst].astype(BF16)
        elif kind == 2:
            a_lat = _pool_core(x, m_lat, mix_pre_g[i])
            a_ctx = _pool_core(xc.reshape(b, lc, d), jnp.broadcast_to(m_ctx, (b, N_MOD, d)), mix_pre_g[i]) if need_ctx else None
            w_out = pool_w[inst].astype(BF16)
            post_kw = dict(pool_bias=pool_b[inst], pool_scale=pool_scale[inst])
        else:
            nh = DN_HEADS
            dk = d // nh
            nqkv = 3 * nh * dk
            wd = dn_w_in[inst]
            w_main = jnp.concatenate([wd[:, :nqkv], wd[:, nqkv + 4 * nh:]], axis=1).astype(BF16)
            w_gate = jnp.zeros((d, LANES), F32).at[:, :4 * nh].set(wd[:, nqkv:nqkv + 4 * nh]).astype(BF16)
            p_lat, g_lat = _inproj(x, m_lat, mix_pre_g[i], w_main, aux_w=w_gate, name="dn_inproj")
            p_ctx, g_ctx = _inproj(xc, m_ctx, mix_pre_g[i], w_main, aux_w=w_gate, name="dn_inproj_ctx")
            p_ctx, g_ctx = p_ctx.reshape(b, lc, -1), g_ctx.reshape(b, lc, -1)
            a_lanes = jnp.zeros((1, LANES), F32).at[0, 2 * nh:4 * nh].set(dn_a_log[inst].reshape(-1))
            dtb_lanes = jnp.zeros((1, LANES), F32).at[0, 2 * nh:4 * nh].set(dn_dt_bias[inst].reshape(-1))
            a_lat, a_ctx = _deltanet_core(p_lat, g_lat, p_ctx, g_ctx, dn_conv_w[inst], a_lanes, dtb_lanes,
                                          dn_norm_g[inst], need_ctx)
            w_out = dn_w_out[inst].astype(BF16)

        x = _post(a_lat, x, m_lat, mix_post_g[i], mlp_pre_g[i], mlp_post_g[i], w_out, w1, w2,
                  name="post", **post_kw)
        if need_ctx:
            xc = _post(a_ctx.reshape(1, b * lc, -1), xc, m_ctx, mix_post_g[i], mlp_pre_g[i], mlp_post_g[i],
                       w_out, w1, w2, name="post_ctx", **post_kw)
    return x
```

```python
import functools

import numpy as np
import jax
import jax.numpy as jnp
from jax import lax
from jax.experimental import pallas as pl
from jax.experimental.pallas import tpu as pltpu

F32 = jnp.float32
BF16 = jnp.bfloat16
HI = lax.Precision.HIGHEST

EPS = 1e-6
NEG_INF = -1e30
ROPE_BASE = 10000.0
GRID_W = 64
N_MOD = 6
RET_HEADS = 8
RET_CHUNK = 128
ATT_HEADS = 16
ATT_KV_HEADS = 4
ATT_WINDOW = 128
ATT_BLOCK = 128
POOL_WINDOWS = (2, 4, 8, 16)
POOL_PAD = 16
DN_HEADS = 8
DN_CHUNK = 64
DN_CONV_W = 5

LANES = 128
VMEM_LIMIT = 48 * 1024 * 1024


def _cparams(sem):
    return pltpu.CompilerParams(dimension_semantics=sem, vmem_limit_bytes=VMEM_LIMIT)


def _dot(a, b):
    return jnp.dot(a, b, preferred_element_type=F32)


def _dot_nt(a, b):
    return lax.dot_general(a, b, (((1,), (1,)), ((), ())), preferred_element_type=F32)


def _dot_hi(a, b):
    return jnp.dot(a, b, preferred_element_type=F32, precision=HI)


def _rms(x):
    return x * lax.rsqrt(jnp.mean(x * x, axis=-1, keepdims=True) + EPS)


def _silu(x):
    return x * jax.nn.sigmoid(x)


def _pick_tile(n, pref):
    t = min(pref, n)
    while n % t:
        t //= 2
    return t


def _mod_kernel(c_ref, w_ref, b_ref, o_ref):
    o_ref[0] = _dot_hi(_silu(c_ref[...]), w_ref[0]) + b_ref[0]


def _modulation(cond, ada_w, ada_b):
    depth, d, n = ada_w.shape
    rows = cond.shape[0]
    tn = _pick_tile(n, 1536)
    return pl.pallas_call(
        _mod_kernel,
        grid=(depth, n // tn),
        in_specs=[pl.BlockSpec((rows, d), lambda i, j: (0, 0)),
                  pl.BlockSpec((1, d, tn), lambda i, j: (i, 0, j)),
                  pl.BlockSpec((1, 1, tn), lambda i, j: (i, 0, j))],
        out_specs=pl.BlockSpec((1, rows, tn), lambda i, j: (i, 0, j)),
        out_shape=jax.ShapeDtypeStruct((depth, rows, n), F32),
        compiler_params=_cparams(("parallel", "parallel")),
        name="modulation",
    )(cond, ada_w, ada_b.reshape(depth, 1, n))


def _inproj_kernel(x_ref, mod_ref, g_ref, w_ref, *rest, has_aux):
    if has_aux:
        wa_ref, o_ref, oa_ref, u_scr = rest
    else:
        o_ref, u_scr = rest

    @pl.when(pl.program_id(2) == 0)
    def _():
        m = mod_ref[0]
        u = _rms(x_ref[0]) * g_ref[...] * (1.0 + m[1:2]) + m[0:1]
        u_scr[...] = u.astype(BF16)
        if has_aux:
            oa_ref[0] = _dot(u_scr[...], wa_ref[...])

    o_ref[0] = _dot(u_scr[...], w_ref[...]).astype(o_ref.dtype)


def _inproj(x, mod, g, w, aux_w=None, name="inproj"):
    bx, lx, d = x.shape
    n = w.shape[1]
    tm = _pick_tile(lx, 1024)
    tn = _pick_tile(n, 512)
    has_aux = aux_w is not None
    in_specs = [pl.BlockSpec((1, tm, d), lambda b, i, j: (b, i, 0)),
                pl.BlockSpec((1, N_MOD, d), lambda b, i, j: (b, 0, 0)),
                pl.BlockSpec((1, d), lambda b, i, j: (0, 0)),
                pl.BlockSpec((d, tn), lambda b, i, j: (0, j))]
    out_specs = [pl.BlockSpec((1, tm, tn), lambda b, i, j: (b, i, j))]
    out_shape = [jax.ShapeDtypeStruct((bx, lx, n), BF16)]
    args = [x, mod, g.reshape(1, d), w]
    if has_aux:
        na = aux_w.shape[1]
        in_specs.append(pl.BlockSpec((d, na), lambda b, i, j: (0, 0)))
        out_specs.append(pl.BlockSpec((1, tm, na), lambda b, i, j: (b, i, 0)))
        out_shape.append(jax.ShapeDtypeStruct((bx, lx, na), F32))
        args.append(aux_w)
    outs = pl.pallas_call(
        functools.partial(_inproj_kernel, has_aux=has_aux),
        grid=(bx, lx // tm, n // tn),
        in_specs=in_specs, out_specs=out_specs, out_shape=out_shape,
        scratch_shapes=[pltpu.VMEM((tm, d), BF16)],
        compiler_params=_cparams(("parallel", "parallel", "arbitrary")),
        name=name,
    )(*args)
    return outs if has_aux else outs[0]


def _post_kernel(a_ref, x_ref, mod_ref, gpost_ref, gpre2_ref, gpost2_ref, wo_ref, *rest, grouped):
    if grouped:
        pb_ref, ps_ref, w1_ref, w2_ref, o_ref, x1_scr, u_scr, acc_scr = rest
    else:
        w1_ref, w2_ref, o_ref, x1_scr, u_scr, acc_scr = rest
    j = pl.program_id(2)

    @pl.when(j == 0)
    def _():
        a = a_ref[0]
        if grouped:
            ng, gw, _ = wo_ref.shape
            y = jnp.concatenate([_dot(a[:, g * gw:(g + 1) * gw], wo_ref[g]) for g in range(ng)], axis=1)
            y = (y + pb_ref[...]) * ps_ref[...]
        else:
            y = _dot(a, wo_ref[...])
        m = mod_ref[0]
        x1 = x_ref[0] + m[2:3] * (_rms(y) * gpost_ref[...])
        x1_scr[...] = x1
        u = _rms(x1) * gpre2_ref[...] * (1.0 + m[4:5]) + m[3:4]
        u_scr[...] = u.astype(BF16)
        acc_scr[...] = jnp.zeros_like(acc_scr)

    h = jnp.square(jnp.maximum(_dot(u_scr[...], w1_ref[...]), 0.0)).astype(BF16)
    acc_scr[...] += _dot(h, w2_ref[...])

    @pl.when(j == pl.num_programs(2) - 1)
    def _():
        m = mod_ref[0]
        o_ref[0] = x1_scr[...] + m[5:6] * (_rms(acc_scr[...]) * gpost2_ref[...])


def _post(a, x, mod, gpost, gpre2, gpost2, w_out, w1, w2, pool_bias=None, pool_scale=None, name="post"):
    bx, lx, d = x.shape
    ka = a.shape[2]
    dff = w1.shape[1]
    tm = _pick_tile(lx, 512)
    tf = _pick_tile(dff, 512)
    grouped = pool_bias is not None
    row = lambda v: v.reshape(1, d)
    const2 = lambda b, i, j: (0, 0)
    in_specs = [pl.BlockSpec((1, tm, ka), lambda b, i, j: (b, i, 0)),
                pl.BlockSpec((1, tm, d), lambda b, i, j: (b, i, 0)),
                pl.BlockSpec((1, N_MOD, d), lambda b, i, j: (b, 0, 0)),
                pl.BlockSpec((1, d), const2), pl.BlockSpec((1, d), const2), pl.BlockSpec((1, d), const2)]
    args = [a, x, mod, row(gpost), row(gpre2), row(gpost2), w_out]
    if grouped:
        in_specs += [pl.BlockSpec(w_out.shape, lambda b, i, j: (0, 0, 0)),
                     pl.BlockSpec((1, d), const2), pl.BlockSpec((1, d), const2)]
        args += [row(pool_bias), row(pool_scale)]
    else:
        in_specs += [pl.BlockSpec(w_out.shape, const2)]
    in_specs += [pl.BlockSpec((d, tf), lambda b, i, j: (0, j)), pl.BlockSpec((tf, d), lambda b, i, j: (j, 0))]
    args += [w1, w2]
    return pl.pallas_call(
        functools.partial(_post_kernel, grouped=grouped),
        grid=(bx, lx // tm, dff // tf),
        in_specs=in_specs,
        out_specs=pl.BlockSpec((1, tm, d), lambda b, i, j: (b, i, 0)),
        out_shape=jax.ShapeDtypeStruct((bx, lx, d), F32),
        scratch_shapes=[pltpu.VMEM((tm, d), F32), pltpu.VMEM((tm, d), BF16), pltpu.VMEM((tm, d), F32)],
        compiler_params=_cparams(("parallel", "parallel", "arbitrary")),
        name=name,
    )(*args)


def _rope_tables(n, head_dim, n_rep, scale=1.0):
    quarter = head_dim // 4
    inv = ROPE_BASE ** (-jnp.arange(quarter, dtype=F32) / quarter)
    t = jnp.arange(n, dtype=jnp.int32)
    ang_r = (t // GRID_W).astype(F32)[:, None] * inv[None, :]
    ang_c = (t % GRID_W).astype(F32)[:, None] * inv[None, :]
    cos = jnp.concatenate([jnp.cos(ang_r), jnp.cos(ang_r), jnp.cos(ang_c), jnp.cos(ang_c)], axis=1)
    sin = jnp.concatenate([-jnp.sin(ang_r), jnp.sin(ang_r), -jnp.sin(ang_c), jnp.sin(ang_c)], axis=1)
    return jnp.tile(cos, (1, n_rep)) * scale, jnp.tile(sin, (1, n_rep)) * scale


def _rope(x, cos, sin, quarter):
    w = x.shape[1]
    lane = lax.broadcasted_iota(jnp.int32, x.shape, 1)
    first = (lane & (2 * quarter - 1)) < quarter
    partner = jnp.where(first, pltpu.roll(x, w - quarter, 1), pltpu.roll(x, quarter, 1))
    return x * cos + partner * sin


def _head_scalar(vec, h):
    lane = lax.broadcasted_iota(jnp.int32, vec.shape, 1)
    return jnp.sum(jnp.where(lane == h, vec, 0.0), axis=1, keepdims=True)


def _log_sigmoid(x):
    return jnp.minimum(x, 0.0) - jnp.log1p(jnp.exp(-jnp.abs(x)))


def _run_bidirectional(n, step, want_out):
    if want_out:
        assert n % 2 == 0
        lax.fori_loop(0, n // 2, lambda i, c: step(i, n - 1 - i, False) or c, 0)
        lax.fori_loop(n // 2, n, lambda i, c: step(i, n - 1 - i, True) or c, 0)
    else:
        lax.fori_loop(0, n, lambda i, c: step(i, n - 1 - i, None) or c, 0)


def _ret_kernel(dl_ref, cos_ref, sin_ref, ql_ref, kl_ref, vl_ref, gfl_ref, gbl_ref,
                qc_ref, kc_ref, vc_ref, gfc_ref, gbc_ref, yl_ref, yc_ref,
                q_scr, k_scr, sf_scr, sb_scr, yacc_scr):
    c = RET_CHUNK
    l = ql_ref.shape[1]
    lc = qc_ref.shape[1]
    dk = ql_ref.shape[2]
    h = pl.program_id(1)
    scale = dk ** -0.5

    ls = _log_sigmoid(dl_ref[...])
    lgf = _head_scalar(ls[0:1], h)
    lgb = _head_scalar(ls[1:2], h)

    q_scr[0:lc, :] = (qc_ref[0].astype(F32) * scale).astype(BF16)
    k_scr[0:lc, :] = kc_ref[0]
    rb = _pick_tile(l, 256)

    def rope_body(r, carry):
        rows = pl.ds(pl.multiple_of(r * rb, rb), rb)
        dst = pl.ds(pl.multiple_of(lc + r * rb, rb), rb)
        cs, sn = cos_ref[rows, :], sin_ref[rows, :]
        q_scr[dst, :] = _rope(ql_ref[0, rows, :].astype(F32) * scale, cs, sn, dk // 4).astype(BF16)
        k_scr[dst, :] = _rope(kl_ref[0, rows, :].astype(F32), cs, sn, dk // 4).astype(BF16)
        return carry

    lax.fori_loop(0, l // rb, rope_body, 0)

    ii = lax.broadcasted_iota(jnp.int32, (c, c), 0).astype(F32)
    jj = lax.broadcasted_iota(jnp.int32, (c, c), 1).astype(F32)
    diff = ii - jj
    dec_f = jnp.where(diff >= 0, jnp.exp(lgf * jnp.maximum(diff, 0.0)), 0.0)
    dec_b = jnp.where(diff <= 0, jnp.exp(lgb * jnp.maximum(-diff, 0.0)), 0.0)
    idx = lax.broadcasted_iota(jnp.int32, (c, 1), 0).astype(F32)
    qdec_f, kdec_f, cd_f = jnp.exp(lgf * (idx + 1.0)), jnp.exp(lgf * (c - 1.0 - idx)), jnp.exp(lgf * c)
    qdec_b, kdec_b, cd_b = jnp.exp(lgb * (c - idx)), jnp.exp(lgb * idx), jnp.exp(lgb * c)

    sf_scr[...] = jnp.zeros_like(sf_scr)
    sb_scr[...] = jnp.zeros_like(sb_scr)

    def one_dir(base, ci, v_ref, g_ref, s_scr, dec, qdec, kdec, cd):
        rows = pl.ds(pl.multiple_of(ci * c, c), c)
        srows = pl.ds(pl.multiple_of(base + ci * c, c), c)
        q, k, v = q_scr[srows, :], k_scr[srows, :], v_ref[0, rows, :]
        s = s_scr[...]
        scores = (_dot_nt(q, k) * dec).astype(BF16)
        o = _dot(scores, v) + qdec * _dot(q, s.astype(BF16))
        kd = (k.astype(F32) * kdec).T.astype(BF16)
        s_scr[...] = s * cd + _dot(kd, v)
        y = o * lax.rsqrt(jnp.mean(o * o, axis=-1, keepdims=True) + EPS)
        return rows, y * _silu(g_ref[0, rows, :].astype(F32))

    def make_step(base, v_ref, gf_ref, gb_ref, y_ref):
        def step(cf, cb, second_half):
            rows_f, yf = one_dir(base, cf, v_ref, gf_ref, sf_scr, dec_f, qdec_f, kdec_f, cd_f)
            rows_b, yb = one_dir(base, cb, v_ref, gb_ref, sb_scr, dec_b, qdec_b, kdec_b, cd_b)
            if second_half:
                y_ref[0, rows_f, :] = (yacc_scr[rows_f, :] + yf).astype(y_ref.dtype)
                y_ref[0, rows_b, :] = (yacc_scr[rows_b, :] + yb).astype(y_ref.dtype)
            else:
                yacc_scr[rows_f, :] = yf
                yacc_scr[rows_b, :] = yb
        return step

    _run_bidirectional(lc // c, make_step(0, vc_ref, gfc_ref, gbc_ref, yc_ref), True)
    _run_bidirectional(l // c, make_step(lc, vl_ref, gfl_ref, gbl_ref, yl_ref), True)


def _retention_core(p_lat, p_ctx, decay_logit, cos, sin):
    b, l, _ = p_lat.shape
    lc = p_ctx.shape[1]
    nh = RET_HEADS
    dk = cos.shape[1]
    dv = 2 * dk

    def specs(n):
        return [pl.BlockSpec((1, n, dk), lambda bi, h: (bi, 0, h)),
                pl.BlockSpec((1, n, dk), lambda bi, h: (bi, 0, nh + h)),
                pl.BlockSpec((1, n, dv), lambda bi, h: (bi, 0, nh + h)),
                pl.BlockSpec((1, n, dv), lambda bi, h: (bi, 0, 2 * nh + h)),
                pl.BlockSpec((1, n, dv), lambda bi, h: (bi, 0, 3 * nh + h))]

    const2 = lambda bi, h: (0, 0)
    return pl.pallas_call(
        _ret_kernel,
        grid=(b, nh),
        in_specs=[pl.BlockSpec(decay_logit.shape, const2), pl.BlockSpec((l, dk), const2),
                  pl.BlockSpec((l, dk), const2)] + specs(l) + specs(lc),
        out_specs=[pl.BlockSpec((1, l, dv), lambda bi, h: (bi, 0, h)),
                   pl.BlockSpec((1, lc, dv), lambda bi, h: (bi, 0, h))],
        out_shape=[jax.ShapeDtypeStruct((b, l, nh * dv), BF16), jax.ShapeDtypeStruct((b, lc, nh * dv), BF16)],
        scratch_shapes=[pltpu.VMEM((lc + l, dk), BF16), pltpu.VMEM((lc + l, dk), BF16),
                        pltpu.VMEM((dk, dv), F32), pltpu.VMEM((dk, dv), F32),
                        pltpu.VMEM((max(l, lc), dv), F32)],
        compiler_params=_cparams(("parallel", "parallel")),
        name="retention_core",
    )(decay_logit, cos, sin, *([p_lat] * 5), *([p_ctx] * 5))


def _sink_softmax_out(parts, sink_col):
    mx = sink_col
    for s, _ in parts:
        mx = jnp.maximum(mx, jnp.max(s, axis=-1, keepdims=True))
    den = jnp.exp(sink_col - mx)
    acc = None
    for s, v in parts:
        p = jnp.exp(s - mx)
        den = den + jnp.sum(p, axis=-1, keepdims=True)
        pv = _dot(p.astype(BF16), v)
        acc = pv if acc is None else acc + pv
    return acc / den


def _att_kernel(sink_ref, cos_ref, sin_ref, pl_ref, pc_ref, ol_ref, *rest, with_ctx_out):
    if with_ctx_out:
        oc_ref, r_scr = rest
    else:
        (r_scr,) = rest
    grp = ATT_HEADS // ATT_KV_HEADS
    qb, win = ATT_BLOCK, ATT_WINDOW
    span = qb + 2 * win
    l = pl_ref.shape[1]
    lc = pc_ref.shape[1]
    dh = pl_ref.shape[2] // (grp + 2)
    nq = grp * dh
    kh = pl.program_id(1)

    rb = _pick_tile(l, 256)

    def rope_body(r, carry):
        rows = pl.ds(pl.multiple_of(r * rb, rb), rb)
        x = pl_ref[0, rows, :].astype(F32)
        r_scr[rows, :] = _rope(x, cos_ref[rows, :], sin_ref[rows, :], dh // 4).astype(BF16)
        return carry

    lax.fori_loop(0, l // rb, rope_body, 0)

    def stack_heads(q):
        return jnp.concatenate([q[:, g * dh:(g + 1) * dh] for g in range(grp)], axis=0)

    def unstack_heads(o, m):
        return jnp.concatenate([o[g * m:(g + 1) * m, :] for g in range(grp)], axis=1)

    def sink_col(m):
        return jnp.concatenate([jnp.full((m, 1), sink_ref[kh * grp + g], F32) for g in range(grp)], axis=0)

    k_ctx = pc_ref[0, :, nq:nq + dh]
    v_ctx = pc_ref[0, :, nq + dh:nq + 2 * dh]
    sink_blk = sink_col(qb)
    qi = lax.broadcasted_iota(jnp.int32, (grp * qb, span), 0) & (qb - 1)
    kj = lax.broadcasted_iota(jnp.int32, (grp * qb, span), 1)

    def block(bi, carry):
        qs = pl.multiple_of(bi * qb, qb)
        ks = pl.multiple_of(jnp.clip(qs - win, 0, l - span), qb)
        q = stack_heads(r_scr[pl.ds(qs, qb), 0:nq])
        kv = r_scr[pl.ds(ks, span), nq:nq + 2 * dh]
        valid = jnp.abs((ks + kj) - (qs + qi)) <= win
        s_loc = jnp.where(valid, _dot_nt(q, kv[:, 0:dh]), NEG_INF)
        s_ctx = _dot_nt(q, k_ctx)
        o = _sink_softmax_out([(s_loc, kv[:, dh:2 * dh]), (s_ctx, v_ctx)], sink_blk)
        ol_ref[0, pl.ds(qs, qb), :] = unstack_heads(o, qb).astype(ol_ref.dtype)
        return carry

    lax.fori_loop(0, l // qb, block, 0)

    if with_ctx_out:
        qc = stack_heads((pc_ref[0, :, 0:nq].astype(F32) * dh ** -0.5).astype(BF16))
        o = _sink_softmax_out([(_dot_nt(qc, k_ctx), v_ctx)], sink_col(lc))
        oc_ref[0] = unstack_heads(o, lc).astype(oc_ref.dtype)


def _attention_core(p_lat, p_ctx, sink, cos, sin, with_ctx_out):
    b, l, n = p_lat.shape
    lc = p_ctx.shape[1]
    nkv = ATT_KV_HEADS
    wblk = n // nkv
    nq = wblk * (ATT_HEADS // nkv) // (ATT_HEADS // nkv + 2)
    const2 = lambda bi, h: (0, 0)
    out_specs = [pl.BlockSpec((1, l, nq), lambda bi, h: (bi, 0, h))]
    out_shape = [jax.ShapeDtypeStruct((b, l, nkv * nq), BF16)]
    if with_ctx_out:
        out_specs.append(pl.BlockSpec((1, lc, nq), lambda bi, h: (bi, 0, h)))
        out_shape.append(jax.ShapeDtypeStruct((b, lc, nkv * nq), BF16))
    outs = pl.pallas_call(
        functools.partial(_att_kernel, with_ctx_out=with_ctx_out),
        grid=(b, nkv),
        in_specs=[pl.BlockSpec(memory_space=pltpu.SMEM),
                  pl.BlockSpec((l, wblk), const2), pl.BlockSpec((l, wblk), const2),
                  pl.BlockSpec((1, l, wblk), lambda bi, h: (bi, 0, h)),
                  pl.BlockSpec((1, lc, wblk), lambda bi, h: (bi, 0, h))],
        out_specs=out_specs, out_shape=out_shape,
        scratch_shapes=[pltpu.VMEM((l, wblk), BF16)],
        compiler_params=_cparams(("parallel", "parallel")),
        name="attention_core",
    )(sink, cos, sin, p_lat, p_ctx)
    return (outs[0], outs[1]) if with_ctx_out else (outs[0], None)


def _pool_kernel(x_ref, mod_ref, g_ref, o_ref, rinv_scr, up_scr):
    n, d = x_ref.shape[1], x_ref.shape[2]
    ng = len(POOL_WINDOWS)
    gw = d // ng
    pad = POOL_PAD
    m = mod_ref[0]
    x = x_ref[0]
    rinv_scr[...] = lax.rsqrt(jnp.mean(x * x, axis=-1, keepdims=True) + EPS)
    up_scr[0:pad, :] = jnp.zeros((pad, gw), F32)
    up_scr[pad + n:pad + n + pad, :] = jnp.zeros((pad, gw), F32)
    t = lax.broadcasted_iota(jnp.int32, (n, 1), 0)
    for gi, w in enumerate(POOL_WINDOWS):
        cols = slice(gi * gw, (gi + 1) * gw)
        u = x_ref[0, :, cols] * rinv_scr[...] * g_ref[:, cols] * (1.0 + m[1:2, cols]) + m[0:1, cols]
        up_scr[pad:pad + n, :] = u
        lo, hi = w // 2, w - 1 - w // 2
        tot = up_scr[pad - lo:pad - lo + n, :]
        for dlt in range(-lo + 1, hi + 1):
            tot = tot + up_scr[pad + dlt:pad + dlt + n, :]
        cnt = (jnp.minimum(t + hi + 1, n) - jnp.maximum(t - lo, 0)).astype(F32)
        o_ref[0, :, cols] = (tot / cnt - u).astype(o_ref.dtype)


def _pool_core(x, mod, g):
    bx, n, d = x.shape
    gw = d // len(POOL_WINDOWS)
    return pl.pallas_call(
        _pool_kernel,
        grid=(bx,),
        in_specs=[pl.BlockSpec((1, n, d), lambda b: (b, 0, 0)),
                  pl.BlockSpec((1, N_MOD, d), lambda b: (b, 0, 0)),
                  pl.BlockSpec((1, d), lambda b: (0, 0))],
        out_specs=pl.BlockSpec((1, n, d), lambda b: (b, 0, 0)),
        out_shape=jax.ShapeDtypeStruct((bx, n, d), BF16),
        scratch_shapes=[pltpu.VMEM((n, 1), F32), pltpu.VMEM((n + 2 * POOL_PAD, gw), F32)],
        compiler_params=_cparams(("parallel",)),
        name="pool_core",
    )(x, mod, g.reshape(1, d))


def _unit_tri_inverses(mats):
    c = mats[0].shape[0]
    ii = lax.broadcasted_iota(jnp.int32, (c, c), 0)
    jj = lax.broadcasted_iota(jnp.int32, (c, c), 1)
    blk = lambda t, k: lax.shift_right_logical(t, k)
    eye = (ii == jj).astype(F32)
    pair = (blk(ii, 1) == blk(jj, 1)) & (ii != jj)
    ds = [eye - jnp.where(pair, a, 0.0) for a in mats]
    k = 1
    while (2 << k) <= c:
        m = (blk(ii, k + 1) == blk(jj, k + 1)) & (blk(ii, k) != blk(jj, k))
        xs = [_dot_t(jnp.where(m, a, 0.0), d) for a, d in zip(mats, ds)]
        ys = [_dot_t(d, x) for d, x in zip(ds, xs)]
        ds = [d - y for d, y in zip(ds, ys)]
        k += 1
    return ds


def _dot_t(a, b):
    return _dot(a.astype(BF16), b.astype(BF16))


def _cumsum_rows(tri, x):
    hi = x.astype(BF16)
    lo = (x - hi.astype(F32)).astype(BF16)
    r = _dot(tri, jnp.concatenate([hi, lo], axis=1))
    return r[:, 0:LANES] + r[:, LANES:2 * LANES]


def _dn_kernel(alog_ref, dtb_ref, ng_ref, cw_ref_q, cw_ref_k, cw_ref_v,
               ql_ref, kl_ref, vl_ref, gfl_ref, gbl_ref, gl_ref,
               qc_ref, kc_ref, vc_ref, gfc_ref, gbc_ref, gc_ref,
               yl_ref, *rest, with_ctx_out):
    if with_ctx_out:
        yc_ref, rest = rest[0], rest[1:]
    else:
        yc_ref = None
    q_scr, k_scr, v_scr, xp_scr, sf_scr, sb_scr, yacc_scr = rest[:7]
    fwd_bufs, bwd_bufs = rest[7:12], rest[12:17]
    c = DN_CHUNK
    nh = DN_HEADS
    l = ql_ref.shape[1]
    lc = qc_ref.shape[1]
    dk = ql_ref.shape[2]
    h = pl.program_id(1)
    kw = DN_CONV_W
    cpad = 8

    def conv_seq(src_ref, cw_ref, dst_scr, base, n, norm, scale):
        xp_scr[0:cpad, :] = jnp.zeros((cpad, dk), F32)
        xp_scr[cpad:cpad + n, :] = src_ref[0].astype(F32)
        xp_scr[cpad + n:cpad + n + cpad, :] = jnp.zeros((cpad, dk), F32)
        rb = _pick_tile(n, 256)

        def body(r, carry):
            start = pl.multiple_of(r * rb, rb)
            win = xp_scr[pl.ds(start, rb + 2 * cpad), :]
            y = None
            for t in range(kw):
                o = cpad - kw // 2 + t
                term = win[o:o + rb, :] * cw_ref[t:t + 1, :]
                y = term if y is None else y + term
            y = _silu(y)
            if norm:
                y = y * lax.rsqrt(jnp.sum(y * y, axis=-1, keepdims=True) + EPS) * scale
            dst_scr[pl.ds(pl.multiple_of(base + start, rb), rb), :] = y.astype(BF16)
            return carry

        lax.fori_loop(0, n // rb, body, 0)

    for src_c, src_l, cw, dst, norm, scale in ((qc_ref, ql_ref, cw_ref_q, q_scr, True, dk ** -0.5),
                                               (kc_ref, kl_ref, cw_ref_k, k_scr, True, 1.0),
                                               (vc_ref, vl_ref, cw_ref_v, v_scr, False, 1.0)):
        conv_seq(src_c, cw, dst, 0, lc, norm, scale)
        conv_seq(src_l, cw, dst, lc, l, norm, scale)

    lane = lax.broadcasted_iota(jnp.int32, (1, LANES), 1)
    neg_a = -jnp.exp(alog_ref[...])
    dtb = dtb_ref[...]
    ii = lax.broadcasted_iota(jnp.int32, (c, c), 0)
    jj = lax.broadcasted_iota(jnp.int32, (c, c), 1)
    tri_l = (ii >= jj).astype(BF16)
    tri_u = (ii <= jj).astype(BF16)
    norm_g = ng_ref[...]

    def lane_col(x, idx):
        return jnp.broadcast_to(jnp.sum(jnp.where(lane == idx, x, 0.0), axis=1, keepdims=True), (c, LANES))

    def prep_group(base, first_ci, g_ref, nb):
        chains = []
        for t in range(nb):
            ci = first_ci + t
            rows = pl.ds(pl.multiple_of(ci * c, c), c)
            gci = base // c + ci
            srows = pl.ds(pl.multiple_of(gci * c, c), c)
            q, k = q_scr[srows, :], k_scr[srows, :]
            qf, kf, vf = q.astype(F32), k.astype(F32), v_scr[srows, :].astype(F32)
            kk, qk = _dot_nt(k, k), _dot_nt(q, k)
            graw = g_ref[0, rows, :]
            sig = jax.nn.sigmoid(graw)
            la_all = neg_a * jax.nn.softplus(graw + dtb)
            for upper, bufs in ((False, fwd_bufs), (True, bwd_bufs)):
                beta = lane_col(sig, h + (nh if upper else 0))
                la = lane_col(la_all, h + (3 * nh if upper else 2 * nh))
                chains.append(dict(upper=upper, bufs=bufs, gci=gci, srows=srows, qf=qf, kf=kf, vf=vf, kk=kk, qk=qk,
                                   beta=beta, la=la))
        for ch in chains:
            ch["g"] = _cumsum_rows(tri_u if ch["upper"] else tri_l, ch["la"])
        for ch in chains:
            g = ch["g"]
            keep = (ii <= jj) if ch["upper"] else (ii >= jj)
            g_row = g.T[0:c, 0:c]
            ch["decay"] = jnp.where(keep, jnp.exp(jnp.where(keep, g[:, 0:c] - g_row, 0.0)), 0.0)
        t_invs = _unit_tri_inverses([jnp.where(ii != jj, ch["kk"] * ch["beta"][:, 0:c] * ch["decay"], 0.0)
                                     for ch in chains])
        sols = []
        for ch, t_inv in zip(chains, t_invs):
            ch["eg"] = jnp.exp(ch["g"])
            rhs = jnp.concatenate([ch["vf"] * ch["beta"], ch["kf"] * (ch["beta"] * ch["eg"])], axis=1).astype(BF16)
            sols.append(_dot(t_inv.astype(BF16), rhs))
        for ch, sol in zip(chains, sols):
            u_scr, wq_scr, kgt_scr, at_scr, gl_scr = ch["bufs"]
            g, gci, srows = ch["g"], ch["gci"], ch["srows"]
            g_last = g[0:1, :] if ch["upper"] else g[c - 1:c, :]
            u_scr[srows, :] = sol[:, 0:dk]
            wq_scr[pl.ds(pl.multiple_of(gci * 2 * c, c), c), :] = sol[:, dk:2 * dk].astype(BF16)
            wq_scr[pl.ds(pl.multiple_of(gci * 2 * c + c, c), c), :] = (ch["qf"] * ch["eg"]).astype(BF16)
            kgt_scr[pl.ds(pl.multiple_of(gci * dk, dk), dk), :] = (ch["kf"] * jnp.exp(g_last - g)).T.astype(BF16)
            at_scr[srows, :] = (ch["qk"] * ch["decay"]).astype(BF16)
            gl_scr[pl.ds(pl.multiple_of(gci * 8, 8), 8), :] = jnp.broadcast_to(jnp.exp(g_last), (8, LANES))

    def prep_phase(base, n, g_ref):
        nb = _pick_tile(n, 4)

        def body(r, carry):
            prep_group(base, r * nb, g_ref, nb)
            return carry

        lax.fori_loop(0, n // nb, body, 0)

    prep_phase(0, lc // c, gc_ref)
    prep_phase(lc, l // c, gl_ref)

    sf_scr[...] = jnp.zeros_like(sf_scr)
    sb_scr[...] = jnp.zeros_like(sb_scr)

    def two_dirs(base, chunk_ids, gate_refs):
        dirs = []
        for ci, gate_ref, s_scr, bufs in zip(chunk_ids, gate_refs, (sf_scr, sb_scr), (fwd_bufs, bwd_bufs)):
            gci = base // c + ci
            dirs.append(dict(rows=pl.ds(pl.multiple_of(ci * c, c), c), gci=gci, gate_ref=gate_ref, s_scr=s_scr,
                             bufs=bufs, srows=pl.ds(pl.multiple_of(gci * c, c), c)))
        for dr in dirs:
            dr["s"] = dr["s_scr"][...]
            wq = dr["bufs"][1][pl.ds(pl.multiple_of(dr["gci"] * 2 * c, 2 * c), 2 * c), :]
            dr["r"] = _dot(wq, dr["s"].astype(BF16))
        for dr in dirs:
            dr["v_new"] = (dr["bufs"][0][dr["srows"], :] - dr["r"][0:c, :]).astype(BF16)
        for dr in dirs:
            gl = dr["bufs"][4][pl.ds(pl.multiple_of(dr["gci"] * 8, 8), 8), :]
            kgt = dr["bufs"][2][pl.ds(pl.multiple_of(dr["gci"] * dk, dk), dk), :]
            dr["s_scr"][...] = dr["s"] * gl[0:1, :] + _dot(kgt, dr["v_new"])
        outs = []
        for dr in dirs:
            if dr["gate_ref"] is None:
                outs += [dr["rows"], None]
                continue
            o = dr["r"][c:2 * c, :] + _dot(dr["bufs"][3][dr["srows"], :], dr["v_new"])
            y = o * lax.rsqrt(jnp.mean(o * o, axis=-1, keepdims=True) + EPS) * norm_g
            outs += [dr["rows"], y * _silu(dr["gate_ref"][0, dr["rows"], :].astype(F32))]
        return outs

    def make_step(base, g_ref, gf_ref, gb_ref, y_ref):
        def step(cf, cb, second_half):
            rows_f, yf, rows_b, yb = two_dirs(base, (cf, cb), (gf_ref, gb_ref))
            if second_half is None:
                return
            if second_half:
                y_ref[0, rows_f, :] = (yacc_scr[rows_f, :] + yf).astype(y_ref.dtype)
                y_ref[0, rows_b, :] = (yacc_scr[rows_b, :] + yb).astype(y_ref.dtype)
            else:
                yacc_scr[rows_f, :] = yf
                yacc_scr[rows_b, :] = yb
        return step

    if with_ctx_out:
        _run_bidirectional(lc // c, make_step(0, gc_ref, gfc_ref, gbc_ref, yc_ref), True)
    else:
        _run_bidirectional(lc // c, make_step(0, gc_ref, None, None, None), False)
    _run_bidirectional(l // c, make_step(lc, gl_ref, gfl_ref, gbl_ref, yl_ref), True)


def _deltanet_core(p_lat, g_lat, p_ctx, g_ctx, conv_w, a_lanes, dtb_lanes, norm_g, with_ctx_out):
    b, l, _ = p_lat.shape
    lc = p_ctx.shape[1]
    nh = DN_HEADS
    dk = norm_g.shape[-1]

    def specs(n):
        sp = [pl.BlockSpec((1, n, dk), lambda bi, h, k=k: (bi, 0, k * nh + h)) for k in range(5)]
        return sp + [pl.BlockSpec((1, n, LANES), lambda bi, h: (bi, 0, 0))]

    const2 = lambda bi, h: (0, 0)
    kw = conv_w.shape[0]
    c = DN_CHUNK
    ltot = lc + l
    nck = ltot // c
    dir_bufs = [pltpu.VMEM((ltot, dk), F32), pltpu.VMEM((2 * ltot, dk), BF16), pltpu.VMEM((nck * dk, c), BF16),
                pltpu.VMEM((ltot, c), BF16), pltpu.VMEM((nck * 8, LANES), F32)]
    out_specs = [pl.BlockSpec((1, l, dk), lambda bi, h: (bi, 0, h))]
    out_shape = [jax.ShapeDtypeStruct((b, l, nh * dk), BF16)]
    if with_ctx_out:
        out_specs.append(pl.BlockSpec((1, lc, dk), lambda bi, h: (bi, 0, h)))
        out_shape.append(jax.ShapeDtypeStruct((b, lc, nh * dk), BF16))
    outs = pl.pallas_call(
        functools.partial(_dn_kernel, with_ctx_out=with_ctx_out),
        grid=(b, nh),
        in_specs=[pl.BlockSpec((1, LANES), const2), pl.BlockSpec((1, LANES), const2), pl.BlockSpec((1, dk), const2)]
                 + [pl.BlockSpec((kw, dk), lambda bi, h, k=k: (0, k * nh + h)) for k in range(3)]
                 + specs(l) + specs(lc),
        out_specs=out_specs, out_shape=out_shape,
        scratch_shapes=[pltpu.VMEM((lc + l, dk), BF16), pltpu.VMEM((lc + l, dk), BF16),
                        pltpu.VMEM((lc + l, dk), BF16), pltpu.VMEM((max(l, lc) + 16, dk), F32),
                        pltpu.VMEM((dk, dk), F32), pltpu.VMEM((dk, dk), F32),
                        pltpu.VMEM((max(l, lc), dk), F32)] + 2 * dir_bufs,
        compiler_params=_cparams(("parallel", "parallel")),
        name="deltanet_core",
    )(a_lanes, dtb_lanes, norm_g.reshape(1, dk), conv_w, conv_w, conv_w,
      *([p_lat] * 5), g_lat, *([p_ctx] * 5), g_ctx)
    return (outs[0], outs[1]) if with_ctx_out else (outs[0], None)


def kernel(x, c, ctx, c_ctx, ada_w, ada_b, mix_pre_g, mix_post_g, mlp_pre_g, mlp_post_g, mlp_w1, mlp_w2, ret_w_in, ret_decay_logit, ret_w_out, att_w_in, att_sink, att_w_out, pool_w, pool_b, pool_scale, dn_w_in, dn_conv_w, dn_a_log, dn_dt_bias, dn_norm_g, dn_w_out):
    b, l, d = x.shape
    lc = ctx.shape[1]
    depth = ada_w.shape[0]
    n_mixers = 4

    rows = -(-(b + 1) // 8) * 8
    cond = jnp.zeros((rows, d), F32).at[:b].set(c).at[b].set(c_ctx)
    mods = _modulation(cond, ada_w, ada_b)
    xc = ctx.reshape(1, b * lc, d)

    for i in range(depth):
        kind, inst = i % n_mixers, i // n_mixers
        need_ctx = i < depth - 1
        m_lat = mods[i, :b].reshape(b, N_MOD, d)
        m_ctx = mods[i, b:b + 1].reshape(1, N_MOD, d)
        w1, w2 = mlp_w1[i].astype(BF16), mlp_w2[i].astype(BF16)
        post_kw = {}
        if kind == 0:
            w_in = ret_w_in[inst].astype(BF16)
            p_lat = _inproj(x, m_lat, mix_pre_g[i], w_in, name="ret_inproj")
            p_ctx = _inproj(xc, m_ctx, mix_pre_g[i], w_in, name="ret_inproj_ctx").reshape(b, lc, -1)
            dk = d // RET_HEADS
            cos, sin = _rope_tables(l, dk, 1)
            a_lat, a_ctx = _retention_core(p_lat, p_ctx, ret_decay_logit[inst], cos, sin)
            w_out = ret_w_out[inst].astype(BF16)
        elif kind == 1:
            grp = ATT_HEADS // ATT_KV_HEADS
            dh = d // ATT_HEADS
            nq, nkv = ATT_HEADS * dh, ATT_KV_HEADS * dh
            wq = att_w_in[inst][:, :nq].reshape(d, ATT_KV_HEADS, grp * dh)
            wk = att_w_in[inst][:, nq:nq + nkv].reshape(d, ATT_KV_HEADS, dh)
            wv = att_w_in[inst][:, nq + nkv:].reshape(d, ATT_KV_HEADS, dh)
            w_in = jnp.concatenate([wq, wk, wv], axis=2).reshape(d, nq + 2 * nkv).astype(BF16)
            p_lat = _inproj(x, m_lat, mix_pre_g[i], w_in, name="att_inproj")
            p_ctx = _inproj(xc, m_ctx, mix_pre_g[i], w_in, name="att_inproj_ctx").reshape(b, lc, -1)
            cos_h, sin_h = _rope_tables(l, dh, 1)
            one, zero = jnp.ones((l, dh), F32), jnp.zeros((l, dh), F32)
            qs = dh ** -0.5
            cos = jnp.concatenate([jnp.tile(cos_h, (1, grp)) * qs, cos_h, one], axis=1)
            sin = jnp.concatenate([jnp.tile(sin_h, (1, grp)) * qs, sin_h, zero], axis=1)
            a_lat, a_ctx = _attention_core(p_lat, p_ctx, att_sink[inst], cos, sin, need_ctx)
            w_out = att_w_out[inst].astype(BF16)
        elif kind == 2:
            a_lat = _pool_core(x, m_lat, mix_pre_g[i])
            a_ctx = _pool_core(xc.reshape(b, lc, d), jnp.broadcast_to(m_ctx, (b, N_MOD, d)), mix_pre_g[i]) if need_ctx else None
            w_out = pool_w[inst].astype(BF16)
            post_kw = dict(pool_bias=pool_b[inst], pool_scale=pool_scale[inst])
        else:
            nh = DN_HEADS
            dk = d // nh
            nqkv = 3 * nh * dk
            wd = dn_w_in[inst]
            w_main = jnp.concatenate([wd[:, :nqkv], wd[:, nqkv + 4 * nh:]], axis=1).astype(BF16)
            w_gate = jnp.zeros((d, LANES), F32).at[:, :4 * nh].set(wd[:, nqkv:nqkv + 4 * nh]).astype(BF16)
            p_lat, g_lat = _inproj(x, m_lat, mix_pre_g[i], w_main, aux_w=w_gate, name="dn_inproj")
            p_ctx, g_ctx = _inproj(xc, m_ctx, mix_pre_g[i], w_main, aux_w=w_gate, name="dn_inproj_ctx")
            p_ctx, g_ctx = p_ctx.reshape(b, lc, -1), g_ctx.reshape(b, lc, -1)
            a_lanes = jnp.zeros((1, LANES), F32).at[0, 2 * nh:4 * nh].set(dn_a_log[inst].reshape(-1))
            dtb_lanes = jnp.zeros((1, LANES), F32).at[0, 2 * nh:4 * nh].set(dn_dt_bias[inst].reshape(-1))
            a_lat, a_ctx = _deltanet_core(p_lat, g_lat, p_ctx, g_ctx, dn_conv_w[inst], a_lanes, dtb_lanes,
                                          dn_norm_g[inst], need_ctx)
            w_out = dn_w_out[inst].astype(BF16)

        x = _post(a_lat, x, m_lat, mix_post_g[i], mlp_pre_g[i], mlp_post_g[i], w_out, w1, w2,
                  name="post", **post_kw)
        if need_ctx:
            xc = _post(a_ctx.reshape(1, b * lc, -1), xc, m_ctx, mix_post_g[i], mlp_pre_g[i], mlp_post_g[i],
                       w_out, w1, w2, name="post_ctx", **post_kw)
    return x
```

```python
import functools

import numpy as np
import jax
import jax.numpy as jnp
from jax import lax
from jax.experimental import pallas as pl
from jax.experimental.pallas import tpu as pltpu

F32 = jnp.float32
BF16 = jnp.bfloat16
HI = lax.Precision.HIGHEST

EPS = 1e-6
NEG_INF = -1e30
ROPE_BASE = 10000.0
GRID_W = 64
N_MOD = 6
RET_HEADS = 8
RET_CHUNK = 128
ATT_HEADS = 16
ATT_KV_HEADS = 4
ATT_WINDOW = 128
ATT_BLOCK = 128
POOL_WINDOWS = (2, 4, 8, 16)
POOL_PAD = 16
DN_HEADS = 8
DN_CHUNK = 64
DN_CONV_W = 5

LANES = 128
VMEM_LIMIT = 48 * 1024 * 1024
DN_VMEM_LIMIT = 56 * 1024 * 1024


def _cparams(sem):
    return pltpu.CompilerParams(dimension_semantics=sem, vmem_limit_bytes=VMEM_LIMIT)


def _dot(a, b):
    return jnp.dot(a, b, preferred_element_type=F32)


def _dot_nt(a, b):
    return lax.dot_general(a, b, (((1,), (1,)), ((), ())), preferred_element_type=F32)


def _dot_hi(a, b):
    return jnp.dot(a, b, preferred_element_type=F32, precision=HI)


def _rms(x):
    return x * lax.rsqrt(jnp.mean(x * x, axis=-1, keepdims=True) + EPS)


def _silu(x):
    return x * jax.nn.sigmoid(x)


def _pick_tile(n, pref):
    t = min(pref, n)
    while n % t:
        t //= 2
    return t


def _mod_kernel(c_ref, w_ref, b_ref, o_ref):
    o_ref[0] = _dot_hi(_silu(c_ref[...]), w_ref[0]) + b_ref[0]


def _modulation(cond, ada_w, ada_b):
    depth, d, n = ada_w.shape
    rows = cond.shape[0]
    tn = _pick_tile(n, 1536)
    return pl.pallas_call(
        _mod_kernel,
        grid=(depth, n // tn),
        in_specs=[pl.BlockSpec((rows, d), lambda i, j: (0, 0)),
                  pl.BlockSpec((1, d, tn), lambda i, j: (i, 0, j)),
                  pl.BlockSpec((1, 1, tn), lambda i, j: (i, 0, j))],
        out_specs=pl.BlockSpec((1, rows, tn), lambda i, j: (i, 0, j)),
        out_shape=jax.ShapeDtypeStruct((depth, rows, n), F32),
        compiler_params=_cparams(("parallel", "parallel")),
        name="modulation",
    )(cond, ada_w, ada_b.reshape(depth, 1, n))


def _inproj_kernel(x_ref, mod_ref, g_ref, w_ref, *rest, has_aux):
    if has_aux:
        wa_ref, o_ref, oa_ref, u_scr = rest
    else:
        o_ref, u_scr = rest

    @pl.when(pl.program_id(2) == 0)
    def _():
        m = mod_ref[0]
        u = _rms(x_ref[0]) * g_ref[...] * (1.0 + m[1:2]) + m[0:1]
        u_scr[...] = u.astype(BF16)
        if has_aux:
            oa_ref[0] = _dot(u_scr[...], wa_ref[...])

    o_ref[0] = _dot(u_scr[...], w_ref[...]).astype(o_ref.dtype)


def _inproj(x, mod, g, w, aux_w=None, name="inproj"):
    bx, lx, d = x.shape
    n = w.shape[1]
    tm = _pick_tile(lx, 1024)
    tn = n if n <= 2048 else _pick_tile(n, 2048)
    has_aux = aux_w is not None
    in_specs = [pl.BlockSpec((1, tm, d), lambda b, i, j: (b, i, 0)),
                pl.BlockSpec((1, N_MOD, d), lambda b, i, j: (b, 0, 0)),
                pl.BlockSpec((1, d), lambda b, i, j: (0, 0)),
                pl.BlockSpec((d, tn), lambda b, i, j: (0, j))]
    out_specs = [pl.BlockSpec((1, tm, tn), lambda b, i, j: (b, i, j))]
    out_shape = [jax.ShapeDtypeStruct((bx, lx, n), BF16)]
    args = [x, mod, g.reshape(1, d), w]
    if has_aux:
        na = aux_w.shape[1]
        in_specs.append(pl.BlockSpec((d, na), lambda b, i, j: (0, 0)))
        out_specs.append(pl.BlockSpec((1, tm, na), lambda b, i, j: (b, i, 0)))
        out_shape.append(jax.ShapeDtypeStruct((bx, lx, na), F32))
        args.append(aux_w)
    outs = pl.pallas_call(
        functools.partial(_inproj_kernel, has_aux=has_aux),
        grid=(bx, lx // tm, n // tn),
        in_specs=in_specs, out_specs=out_specs, out_shape=out_shape,
        scratch_shapes=[pltpu.VMEM((tm, d), BF16)],
        compiler_params=_cparams(("parallel", "parallel", "arbitrary")),
        name=name,
    )(*args)
    return outs if has_aux else outs[0]


def _post_kernel(a_ref, x_ref, mod_ref, gpost_ref, gpre2_ref, gpost2_ref, wo_ref, *rest, grouped, tf):
    if grouped:
        pb_ref, ps_ref, w1_ref, w2_ref, o_ref = rest
    else:
        w1_ref, w2_ref, o_ref = rest
    a = a_ref[0]
    if grouped:
        ng, gw, _ = wo_ref.shape
        y = jnp.concatenate([_dot(a[:, g * gw:(g + 1) * gw], wo_ref[g]) for g in range(ng)], axis=1)
        y = (y + pb_ref[...]) * ps_ref[...]
    else:
        y = _dot(a, wo_ref[...])
    m = mod_ref[0]
    x1 = x_ref[0] + m[2:3] * (_rms(y) * gpost_ref[...])
    u = (_rms(x1) * gpre2_ref[...] * (1.0 + m[4:5]) + m[3:4]).astype(BF16)
    acc = None
    for j in range(w1_ref.shape[1] // tf):
        h = jnp.square(jnp.maximum(_dot(u, w1_ref[:, j * tf:(j + 1) * tf]), 0.0)).astype(BF16)
        t = _dot(h, w2_ref[j * tf:(j + 1) * tf, :])
        acc = t if acc is None else acc + t
    o_ref[0] = x1 + m[5:6] * (_rms(acc) * gpost2_ref[...])


def _post(a, x, mod, gpost, gpre2, gpost2, w_out, w1, w2, pool_bias=None, pool_scale=None, name="post"):
    bx, lx, d = x.shape
    ka = a.shape[2]
    dff = w1.shape[1]
    tm = _pick_tile(lx, 512)
    tf = _pick_tile(dff, 512)
    grouped = pool_bias is not None
    row = lambda v: v.reshape(1, d)
    const2 = lambda b, i: (0, 0)
    once = pl.Buffered(1)
    in_specs = [pl.BlockSpec((1, tm, ka), lambda b, i: (b, i, 0)),
                pl.BlockSpec((1, tm, d), lambda b, i: (b, i, 0)),
                pl.BlockSpec((1, N_MOD, d), lambda b, i: (b, 0, 0)),
                pl.BlockSpec((1, d), const2), pl.BlockSpec((1, d), const2), pl.BlockSpec((1, d), const2)]
    args = [a, x, mod, row(gpost), row(gpre2), row(gpost2), w_out]
    if grouped:
        in_specs += [pl.BlockSpec(w_out.shape, lambda b, i: (0, 0, 0), pipeline_mode=once),
                     pl.BlockSpec((1, d), const2), pl.BlockSpec((1, d), const2)]
        args += [row(pool_bias), row(pool_scale)]
    else:
        in_specs += [pl.BlockSpec(w_out.shape, const2, pipeline_mode=once)]
    in_specs += [pl.BlockSpec((d, dff), const2, pipeline_mode=once), pl.BlockSpec((dff, d), const2, pipeline_mode=once)]
    args += [w1, w2]
    return pl.pallas_call(
        functools.partial(_post_kernel, grouped=grouped, tf=tf),
        grid=(bx, lx // tm),
        in_specs=in_specs,
        out_specs=pl.BlockSpec((1, tm, d), lambda b, i: (b, i, 0)),
        out_shape=jax.ShapeDtypeStruct((bx, lx, d), F32),
        compiler_params=_cparams(("parallel", "parallel")),
        name=name,
    )(*args)


def _rope_tables(n, head_dim, n_rep, scale=1.0):
    quarter = head_dim // 4
    inv = ROPE_BASE ** (-jnp.arange(quarter, dtype=F32) / quarter)
    t = jnp.arange(n, dtype=jnp.int32)
    ang_r = (t // GRID_W).astype(F32)[:, None] * inv[None, :]
    ang_c = (t % GRID_W).astype(F32)[:, None] * inv[None, :]
    cos = jnp.concatenate([jnp.cos(ang_r), jnp.cos(ang_r), jnp.cos(ang_c), jnp.cos(ang_c)], axis=1)
    sin = jnp.concatenate([-jnp.sin(ang_r), jnp.sin(ang_r), -jnp.sin(ang_c), jnp.sin(ang_c)], axis=1)
    return jnp.tile(cos, (1, n_rep)) * scale, jnp.tile(sin, (1, n_rep)) * scale


def _rope(x, cos, sin, quarter):
    r = lax.broadcasted_iota(jnp.int32, (LANES, LANES), 0)
    col = lax.broadcasted_iota(jnp.int32, (LANES, LANES), 1)
    partner = jnp.where((col & (2 * quarter - 1)) < quarter, col + quarter, col - quarter)
    perm = (r == partner).astype(BF16)
    swapped = jnp.concatenate([_dot(x[:, g * LANES:(g + 1) * LANES], perm) for g in range(x.shape[1] // LANES)], axis=1)
    return x.astype(F32) * cos + swapped * sin


def _head_scalar(vec, h):
    lane = lax.broadcasted_iota(jnp.int32, vec.shape, 1)
    return jnp.sum(jnp.where(lane == h, vec, 0.0), axis=1, keepdims=True)


def _log_sigmoid(x):
    return jnp.minimum(x, 0.0) - jnp.log1p(jnp.exp(-jnp.abs(x)))


def _run_bidirectional(n, step, want_out):
    unroll = 2 if n % 4 == 0 else 1
    if want_out:
        assert n % 2 == 0
        lax.fori_loop(0, n // 2, lambda i, c: step(i, n - 1 - i, False) or c, 0, unroll=unroll)
        lax.fori_loop(n // 2, n, lambda i, c: step(i, n - 1 - i, True) or c, 0, unroll=unroll)
    else:
        lax.fori_loop(0, n, lambda i, c: step(i, n - 1 - i, None) or c, 0, unroll=unroll)


def _ret_kernel(dl_ref, cos_ref, sin_ref, ql_ref, kl_ref, vl_ref, gfl_ref, gbl_ref,
                qc_ref, kc_ref, vc_ref, gfc_ref, gbc_ref, yl_ref, yc_ref,
                q_scr, k_scr, v_scr, p_scr, s_all_scr):
    c = RET_CHUNK
    l = ql_ref.shape[1]
    lc = qc_ref.shape[1]
    dk = ql_ref.shape[2]
    h = pl.program_id(1)
    scale = dk ** -0.5

    ls = _log_sigmoid(dl_ref[...])
    lgf = _head_scalar(ls[0:1], h)
    lgb = _head_scalar(ls[1:2], h)

    q_scr[0:lc, :] = (qc_ref[0].astype(F32) * scale).astype(BF16)
    k_scr[0:lc, :] = kc_ref[0]
    rb = _pick_tile(l, 256)

    def rope_body(r, carry):
        rows = pl.ds(pl.multiple_of(r * rb, rb), rb)
        dst = pl.ds(pl.multiple_of(lc + r * rb, rb), rb)
        cs, sn = cos_ref[rows, :], sin_ref[rows, :]
        q_scr[dst, :] = (_rope(ql_ref[0, rows, :], cs, sn, dk // 4) * scale).astype(BF16)
        k_scr[dst, :] = _rope(kl_ref[0, rows, :], cs, sn, dk // 4).astype(BF16)
        return carry

    lax.fori_loop(0, l // rb, rope_body, 0)

    ii = lax.broadcasted_iota(jnp.int32, (c, c), 0).astype(F32)
    jj = lax.broadcasted_iota(jnp.int32, (c, c), 1).astype(F32)
    diff = ii - jj
    dec_f = jnp.where(diff >= 0, jnp.exp(lgf * jnp.maximum(diff, 0.0)), 0.0)
    dec_b = jnp.where(diff <= 0, jnp.exp(lgb * jnp.maximum(-diff, 0.0)), 0.0)
    idx = lax.broadcasted_iota(jnp.int32, (c, 1), 0).astype(F32)
    dv = vl_ref.shape[2]
    qdec_f, cd_f = jnp.exp(lgf * (idx + 1.0)), jnp.exp(lgf * c)
    qdec_b, cd_b = jnp.exp(lgb * (c - idx)), jnp.exp(lgb * c)

    n_c, n_l = lc // c, l // c
    n = n_c + n_l
    v_scr[0:lc, :] = vc_ref[0]
    v_scr[lc:lc + l, :] = vl_ref[0]

    lane_idx = lax.broadcasted_iota(jnp.int32, (1, c), 1).astype(F32)
    kdec_f_row, kdec_b_row = jnp.exp(lgf * (c - 1.0 - lane_idx)), jnp.exp(lgb * lane_idx)

    grp = _pick_tile(n_l, 4)

    def incr_group(cis):
        srows = [pl.ds(pl.multiple_of(ci * c, c), c) for ci in cis]
        kts = [k_scr[sr, :].astype(F32).T for sr in srows]
        lhs = [jnp.concatenate([kt * kdec_f_row, kt * kdec_b_row], axis=0).astype(BF16) for kt in kts]
        ps = [_dot(a, v_scr[sr, :]) for a, sr in zip(lhs, srows)]
        for ci, p in zip(cis, ps):
            p_scr[pl.ds(pl.multiple_of(ci * 2 * dk, 2 * dk), 2 * dk), :] = p

    def incr_body(r, carry):
        incr_group([n_c + r * grp + t for t in range(grp)])
        return carry

    incr_group(list(range(n_c)))
    lax.fori_loop(0, n_l // grp, incr_body, 0)

    def scan(first, count, step, s, half, cd):
        def body(t, s):
            ci = first + t * step
            s_all_scr[pl.ds(pl.multiple_of(ci * dk, dk), dk), half * dv:(half + 1) * dv] = s.astype(BF16)
            return s * cd + p_scr[pl.ds(pl.multiple_of((ci * 2 + half) * dk, dk), dk), :]
        return lax.fori_loop(0, count, body, s)

    zero = jnp.zeros((dk, dv), F32)
    scan(0, n, 1, zero, 0, cd_f)
    scan(n - 1, n_l, -1, scan(n_c - 1, n_c, -1, zero, 1, cd_b), 1, cd_b)

    def out_group(base_chunk, cis, gf_ref, gb_ref, y_ref):
        srows = [pl.ds(pl.multiple_of((base_chunk + ci) * c, c), c) for ci in cis]
        qs = [q_scr[sr, :] for sr in srows]
        ss = [_dot_nt(q, k_scr[sr, :]) for q, sr in zip(qs, srows)]
        cross = [_dot(q, s_all_scr[pl.ds(pl.multiple_of((base_chunk + ci) * dk, dk), dk), :]) for q, ci in zip(qs, cis)]
        intra = [_dot(jnp.concatenate([s * dec_f, s * dec_b], axis=0).astype(BF16), v_scr[sr, :])
                 for s, sr in zip(ss, srows)]
        for ci, it, cr in zip(cis, intra, cross):
            rows = pl.ds(pl.multiple_of(ci * c, c), c)
            y = None
            for half, (qdec, g_ref) in enumerate(((qdec_f, gf_ref), (qdec_b, gb_ref))):
                o = it[half * c:(half + 1) * c, :] + qdec * cr[:, half * dv:(half + 1) * dv]
                t = o * lax.rsqrt(jnp.mean(o * o, axis=-1, keepdims=True) + EPS) * _silu(g_ref[0, rows, :].astype(F32))
                y = t if y is None else y + t
            y_ref[0, rows, :] = y.astype(y_ref.dtype)

    out_group(0, list(range(n_c)), gfc_ref, gbc_ref, yc_ref)

    def out_body(r, carry):
        out_group(n_c, [r * grp + t for t in range(grp)], gfl_ref, gbl_ref, yl_ref)
        return carry

    lax.fori_loop(0, n_l // grp, out_body, 0)


def _retention_core(p_lat, p_ctx, decay_logit, cos, sin):
    b, l, _ = p_lat.shape
    lc = p_ctx.shape[1]
    nh = RET_HEADS
    dk = cos.shape[1]
    dv = 2 * dk

    def specs(n):
        return [pl.BlockSpec((1, n, dk), lambda bi, h: (bi, 0, h)),
                pl.BlockSpec((1, n, dk), lambda bi, h: (bi, 0, nh + h)),
                pl.BlockSpec((1, n, dv), lambda bi, h: (bi, 0, nh + h)),
                pl.BlockSpec((1, n, dv), lambda bi, h: (bi, 0, 2 * nh + h)),
                pl.BlockSpec((1, n, dv), lambda bi, h: (bi, 0, 3 * nh + h))]

    const2 = lambda bi, h: (0, 0)
    return pl.pallas_call(
        _ret_kernel,
        grid=(b, nh),
        in_specs=[pl.BlockSpec(decay_logit.shape, const2), pl.BlockSpec((l, dk), const2),
                  pl.BlockSpec((l, dk), const2)] + specs(l) + specs(lc),
        out_specs=[pl.BlockSpec((1, l, dv), lambda bi, h: (bi, 0, h)),
                   pl.BlockSpec((1, lc, dv), lambda bi, h: (bi, 0, h))],
        out_shape=[jax.ShapeDtypeStruct((b, l, nh * dv), BF16), jax.ShapeDtypeStruct((b, lc, nh * dv), BF16)],
        scratch_shapes=[pltpu.VMEM((lc + l, dk), BF16), pltpu.VMEM((lc + l, dk), BF16),
                        pltpu.VMEM((lc + l, dv), BF16),
                        pltpu.VMEM(((lc + l) // RET_CHUNK * 2 * dk, dv), F32),
                        pltpu.VMEM(((lc + l) // RET_CHUNK * dk, 2 * dv), BF16)],
        compiler_params=_cparams(("parallel", "parallel")),
        name="retention_core",
    )(decay_logit, cos, sin, *([p_lat] * 5), *([p_ctx] * 5))


def _att_kernel(sink_ref, cos_ref, sin_ref, pl_ref, pc_ref, ol_ref, *rest, with_ctx_out):
    if with_ctx_out:
        oc_ref, rest = rest[0], rest[1:]
    q_scr, k_scr, v_scr, kc_scr, vc_scr, bias_scr = rest
    grp = ATT_HEADS // ATT_KV_HEADS
    qb, win = ATT_BLOCK, ATT_WINDOW
    span = qb + 2 * win
    l = pl_ref.shape[1]
    lc = pc_ref.shape[1]
    dh = pl_ref.shape[2] // (grp + 2)
    nq = grp * dh
    nblk = l // qb
    npair = nq // LANES
    kh = pl.program_id(1)
    assert 2 * dh == LANES

    def split_kv(kv, k_dst, v_dst, rows):
        lo = lax.broadcasted_iota(jnp.int32, kv.shape, 1) < dh
        vk = pltpu.roll(kv, dh, 1)
        one = jnp.ones_like(kv)
        k_dst[0, rows, :] = jnp.where(lo, kv, 0.0).astype(BF16)
        k_dst[1, rows, :] = jnp.where(lo, 0.0, vk).astype(BF16)
        v_dst[0, rows, :] = jnp.concatenate([jnp.where(lo, vk, 0.0), jnp.where(lo, one, 0.0)], axis=1).astype(BF16)
        v_dst[1, rows, :] = jnp.concatenate([jnp.where(lo, 0.0, kv), jnp.where(lo, 0.0, one)], axis=1).astype(BF16)

    rb = _pick_tile(l, 256)

    def rope_body(r, carry):
        rows = pl.ds(pl.multiple_of(r * rb, rb), rb)
        x = _rope(pl_ref[0, rows, :], cos_ref[rows, :], sin_ref[rows, :], dh // 4)
        q_scr[rows, :] = x[:, 0:nq].astype(BF16)
        split_kv(x[:, nq:nq + 2 * dh], k_scr, v_scr, rows)
        return carry

    lax.fori_loop(0, l // rb, rope_body, 0)
    split_kv(pc_ref[0, :, nq:nq + 2 * dh].astype(F32), kc_scr, vc_scr, slice(None))

    qi = lax.broadcasted_iota(jnp.int32, (qb, span), 0)
    kj = lax.broadcasted_iota(jnp.int32, (qb, span), 1)
    for e in range(3):
        bias_scr[e] = jnp.where(jnp.abs(kj - e * win - qi) <= win, 0.0, NEG_INF)
    lo_half = lax.broadcasted_iota(jnp.int32, (1, LANES), 1) < dh

    def attend(problems):
        scores = []
        for q, _, parts in problems:
            scores.append([[_dot_nt(q, kk[par]) if bias is None else _dot_nt(q, kk[par]) + bias
                            for kk, _, bias in parts] for par in range(2)])
        mxs = []
        for (_, pair, _), sc in zip(problems, scores):
            row = []
            for par in range(2):
                mx = jnp.full((sc[par][0].shape[0], 1), sink_ref[kh * grp + 2 * pair + par], F32)
                for s in sc[par]:
                    mx = jnp.maximum(mx, jnp.max(s, axis=-1, keepdims=True))
                row.append(mx)
            mxs.append(row)
        outs = []
        for (_, pair, parts), sc, mx in zip(problems, scores, mxs):
            acc = None
            for par in range(2):
                for s, (_, vv, _) in zip(sc[par], parts):
                    t = _dot(jnp.exp(s - mx[par]).astype(BF16), vv[par])
                    acc = t if acc is None else acc + t
            sink_e = jnp.exp(sink_ref[kh * grp + 2 * pair] - mx[0])
            sink_o = jnp.exp(sink_ref[kh * grp + 2 * pair + 1] - mx[1])
            outs.append(acc[:, 0:LANES] / (acc[:, LANES:2 * LANES] + jnp.where(lo_half, sink_e, sink_o)))
        return outs

    kc_parts = ([kc_scr[0], kc_scr[1]], [vc_scr[0], vc_scr[1]])
    nb = 2 - nblk % 2

    def block_group(r, carry):
        problems, dests = [], []
        for t in range(nb):
            bi = r * nb + t
            qs = pl.multiple_of(bi * qb, qb)
            ks = pl.multiple_of(jnp.clip(qs - win, 0, l - span), qb)
            edge = jnp.where(bi == 0, 0, jnp.where(bi == nblk - 1, 2, 1))
            bias = bias_scr[edge]
            kwin = [k_scr[par, pl.ds(ks, span), :] for par in range(2)]
            vwin = [v_scr[par, pl.ds(ks, span), :] for par in range(2)]
            for pair in range(npair):
                q = q_scr[pl.ds(qs, qb), pair * LANES:(pair + 1) * LANES]
                problems.append((q, pair, [(kwin, vwin, bias), (kc_parts[0], kc_parts[1], None)]))
                dests.append((qs, pair))
        for (qs, pair), o in zip(dests, attend(problems)):
            ol_ref[0, pl.ds(qs, qb), pair * LANES:(pair + 1) * LANES] = o.astype(ol_ref.dtype)
        return carry

    lax.fori_loop(0, nblk // nb, block_group, 0)

    if with_ctx_out:
        problems = []
        for pair in range(npair):
            qc = (pc_ref[0, :, pair * LANES:(pair + 1) * LANES].astype(F32) * dh ** -0.5).astype(BF16)
            problems.append((qc, pair, [(kc_parts[0], kc_parts[1], None)]))
        for pair, o in enumerate(attend(problems)):
            oc_ref[0, :, pair * LANES:(pair + 1) * LANES] = o.astype(oc_ref.dtype)


def _attention_core(p_lat, p_ctx, sink, cos, sin, with_ctx_out):
    b, l, n = p_lat.shape
    lc = p_ctx.shape[1]
    nkv = ATT_KV_HEADS
    wblk = n // nkv
    nq = wblk * (ATT_HEADS // nkv) // (ATT_HEADS // nkv + 2)
    const2 = lambda bi, h: (0, 0)
    out_specs = [pl.BlockSpec((1, l, nq), lambda bi, h: (bi, 0, h))]
    out_shape = [jax.ShapeDtypeStruct((b, l, nkv * nq), BF16)]
    if with_ctx_out:
        out_specs.append(pl.BlockSpec((1, lc, nq), lambda bi, h: (bi, 0, h)))
        out_shape.append(jax.ShapeDtypeStruct((b, lc, nkv * nq), BF16))
    outs = pl.pallas_call(
        functools.partial(_att_kernel, with_ctx_out=with_ctx_out),
        grid=(b, nkv),
        in_specs=[pl.BlockSpec(memory_space=pltpu.SMEM),
                  pl.BlockSpec((l, wblk), const2), pl.BlockSpec((l, wblk), const2),
                  pl.BlockSpec((1, l, wblk), lambda bi, h: (bi, 0, h)),
                  pl.BlockSpec((1, lc, wblk), lambda bi, h: (bi, 0, h))],
        out_specs=out_specs, out_shape=out_shape,
        scratch_shapes=[pltpu.VMEM((l, nq), BF16), pltpu.VMEM((2, l, LANES), BF16), pltpu.VMEM((2, l, 2 * LANES), BF16),
                        pltpu.VMEM((2, lc, LANES), BF16), pltpu.VMEM((2, lc, 2 * LANES), BF16),
                        pltpu.VMEM((3, ATT_BLOCK, ATT_BLOCK + 2 * ATT_WINDOW), F32)],
        compiler_params=_cparams(("parallel", "parallel")),
        name="attention_core",
    )(sink, cos, sin, p_lat, p_ctx)
    return (outs[0], outs[1]) if with_ctx_out else (outs[0], None)


def _pool_kernel(x_ref, mod_ref, g_ref, o_ref, rinv_scr, up_scr):
    n, d = x_ref.shape[1], x_ref.shape[2]
    ng = len(POOL_WINDOWS)
    gw = d // ng
    pad = POOL_PAD
    m = mod_ref[0]
    x = x_ref[0]
    rinv_scr[...] = lax.rsqrt(jnp.mean(x * x, axis=-1, keepdims=True) + EPS)
    up_scr[0:pad, :] = jnp.zeros((pad, gw), F32)
    up_scr[pad + n:pad + n + pad, :] = jnp.zeros((pad, gw), F32)
    t = lax.broadcasted_iota(jnp.int32, (n, 1), 0)
    for gi, w in enumerate(POOL_WINDOWS):
        cols = slice(gi * gw, (gi + 1) * gw)
        u = x_ref[0, :, cols] * rinv_scr[...] * g_ref[:, cols] * (1.0 + m[1:2, cols]) + m[0:1, cols]
        up_scr[pad:pad + n, :] = u
        lo, hi = w // 2, w - 1 - w // 2
        tot = up_scr[pad - lo:pad - lo + n, :]
        for dlt in range(-lo + 1, hi + 1):
            tot = tot + up_scr[pad + dlt:pad + dlt + n, :]
        cnt = (jnp.minimum(t + hi + 1, n) - jnp.maximum(t - lo, 0)).astype(F32)
        o_ref[0, :, cols] = (tot / cnt - u).astype(o_ref.dtype)


def _pool_core(x, mod, g):
    bx, n, d = x.shape
    gw = d // len(POOL_WINDOWS)
    return pl.pallas_call(
        _pool_kernel,
        grid=(bx,),
        in_specs=[pl.BlockSpec((1, n, d), lambda b: (b, 0, 0)),
                  pl.BlockSpec((1, N_MOD, d), lambda b: (b, 0, 0)),
                  pl.BlockSpec((1, d), lambda b: (0, 0))],
        out_specs=pl.BlockSpec((1, n, d), lambda b: (b, 0, 0)),
        out_shape=jax.ShapeDtypeStruct((bx, n, d), BF16),
        scratch_shapes=[pltpu.VMEM((n, 1), F32), pltpu.VMEM((n + 2 * POOL_PAD, gw), F32)],
        compiler_params=_cparams(("parallel",)),
        name="pool_core",
    )(x, mod, g.reshape(1, d))


def _unit_tri_inverses(mats):
    c = mats[0].shape[0]
    ii = lax.broadcasted_iota(jnp.int32, (c, c), 0)
    jj = lax.broadcasted_iota(jnp.int32, (c, c), 1)
    blk = lambda t, k: lax.shift_right_logical(t, k)
    eye = (ii == jj).astype(F32)
    pair = (blk(ii, 1) == blk(jj, 1)) & (ii != jj)
    ds = [eye - jnp.where(pair, a, 0.0) for a in mats]
    k = 1
    while (2 << k) <= c:
        m = (blk(ii, k + 1) == blk(jj, k + 1)) & (blk(ii, k) != blk(jj, k))
        xs = [_dot_t(jnp.where(m, a, 0.0), d) for a, d in zip(mats, ds)]
        ys = [_dot_t(d, x) for d, x in zip(ds, xs)]
        ds = [d - y for d, y in zip(ds, ys)]
        k += 1
    return ds


def _dot_t(a, b):
    return _dot(a.astype(BF16), b.astype(BF16))


def _cumsum_rows(tri, x):
    hi = x.astype(BF16)
    lo = (x - hi.astype(F32)).astype(BF16)
    r = _dot(tri, jnp.concatenate([hi, lo], axis=1))
    return r[:, 0:LANES] + r[:, LANES:2 * LANES]


def _dn_kernel(alog_ref, dtb_ref, ng_ref, cw_ref_q, cw_ref_k, cw_ref_v,
               ql_ref, kl_ref, vl_ref, gfl_ref, gbl_ref, gl_ref,
               qc_ref, kc_ref, vc_ref, gfc_ref, gbc_ref, gc_ref,
               yl_ref, *rest, with_ctx_out):
    if with_ctx_out:
        yc_ref, rest = rest[0], rest[1:]
    else:
        yc_ref = None
    q_scr, k_scr, v_scr, xp_scr, sf_scr, sb_scr, yacc_scr = rest[:7]
    fwd_bufs, bwd_bufs = rest[7:12], rest[12:17]
    c = DN_CHUNK
    nh = DN_HEADS
    l = ql_ref.shape[1]
    lc = qc_ref.shape[1]
    dk = ql_ref.shape[2]
    h = pl.program_id(1)
    kw = DN_CONV_W
    cpad = 8

    def conv_seq(src_ref, cw_ref, dst_scr, base, n, norm, scale):
        xp_scr[0:cpad, :] = jnp.zeros((cpad, dk), F32)
        xp_scr[cpad:cpad + n, :] = src_ref[0].astype(F32)
        xp_scr[cpad + n:cpad + n + cpad, :] = jnp.zeros((cpad, dk), F32)
        rb = _pick_tile(n, 256)

        def body(r, carry):
            start = pl.multiple_of(r * rb, rb)
            win = xp_scr[pl.ds(start, rb + 2 * cpad), :]
            y = None
            for t in range(kw):
                o = cpad - kw // 2 + t
                term = win[o:o + rb, :] * cw_ref[t:t + 1, :]
                y = term if y is None else y + term
            y = _silu(y)
            if norm:
                y = y * lax.rsqrt(jnp.sum(y * y, axis=-1, keepdims=True) + EPS) * scale
            dst_scr[pl.ds(pl.multiple_of(base + start, rb), rb), :] = y.astype(BF16)
            return carry

        lax.fori_loop(0, n // rb, body, 0)

    for src_c, src_l, cw, dst, norm, scale in ((qc_ref, ql_ref, cw_ref_q, q_scr, True, dk ** -0.5),
                                               (kc_ref, kl_ref, cw_ref_k, k_scr, True, 1.0),
                                               (vc_ref, vl_ref, cw_ref_v, v_scr, False, 1.0)):
        conv_seq(src_c, cw, dst, 0, lc, norm, scale)
        conv_seq(src_l, cw, dst, lc, l, norm, scale)

    lane = lax.broadcasted_iota(jnp.int32, (1, LANES), 1)
    neg_a = -jnp.exp(alog_ref[...])
    dtb = dtb_ref[...]
    ii = lax.broadcasted_iota(jnp.int32, (c, c), 0)
    jj = lax.broadcasted_iota(jnp.int32, (c, c), 1)
    tri_l = (ii >= jj).astype(BF16)
    tri_u = (ii <= jj).astype(BF16)
    norm_g = ng_ref[...]

    def lane_col(x, idx):
        return jnp.broadcast_to(jnp.sum(jnp.where(lane == idx, x, 0.0), axis=1, keepdims=True), (c, LANES))

    def prep_group(base, first_ci, g_ref, nb):
        chains = []
        for t in range(nb):
            ci = first_ci + t
            rows = pl.ds(pl.multiple_of(ci * c, c), c)
            gci = base // c + ci
            srows = pl.ds(pl.multiple_of(gci * c, c), c)
            q, k = q_scr[srows, :], k_scr[srows, :]
            qf, kf, vf = q.astype(F32), k.astype(F32), v_scr[srows, :].astype(F32)
            kk, qk = _dot_nt(k, k), _dot_nt(q, k)
            graw = g_ref[0, rows, :]
            sig = jax.nn.sigmoid(graw)
            la_all = neg_a * jax.nn.softplus(graw + dtb)
            for upper, bufs in ((False, fwd_bufs), (True, bwd_bufs)):
                beta = lane_col(sig, h + (nh if upper else 0))
                la = lane_col(la_all, h + (3 * nh if upper else 2 * nh))
                chains.append(dict(upper=upper, bufs=bufs, gci=gci, srows=srows, qf=qf, kf=kf, vf=vf, kk=kk, qk=qk,
                                   beta=beta, la=la))
        for ch in chains:
            ch["g"] = _cumsum_rows(tri_u if ch["upper"] else tri_l, ch["la"])
        for ch in chains:
            g = ch["g"]
            keep = (ii <= jj) if ch["upper"] else (ii >= jj)
            g_row = g.T[0:c, 0:c]
            ch["decay"] = jnp.where(keep, jnp.exp(jnp.where(keep, g[:, 0:c] - g_row, 0.0)), 0.0)
        t_invs = _unit_tri_inverses([jnp.where(ii != jj, ch["kk"] * ch["beta"][:, 0:c] * ch["decay"], 0.0)
                                     for ch in chains])
        sols = []
        for ch, t_inv in zip(chains, t_invs):
            ch["eg"] = jnp.exp(ch["g"])
            rhs = jnp.concatenate([ch["vf"] * ch["beta"], ch["kf"] * (ch["beta"] * ch["eg"])], axis=1).astype(BF16)
            sols.append(_dot(t_inv.astype(BF16), rhs))
        for ch, sol in zip(chains, sols):
            u_scr, wq_scr, kgt_scr, at_scr, gl_scr = ch["bufs"]
            g, gci, srows = ch["g"], ch["gci"], ch["srows"]
            g_last = g[0:1, :] if ch["upper"] else g[c - 1:c, :]
            u_scr[srows, :] = sol[:, 0:dk]
            wq_scr[pl.ds(pl.multiple_of(gci * 2 * c, c), c), :] = sol[:, dk:2 * dk].astype(BF16)
            wq_scr[pl.ds(pl.multiple_of(gci * 2 * c + c, c), c), :] = (ch["qf"] * ch["eg"]).astype(BF16)
            kgt_scr[pl.ds(pl.multiple_of(gci * dk, dk), dk), :] = (ch["kf"] * jnp.exp(g_last - g)).T.astype(BF16)
            at_scr[srows, :] = (ch["qk"] * ch["decay"]).astype(BF16)
            gl_scr[pl.ds(pl.multiple_of(gci * 8, 8), 8), :] = jnp.broadcast_to(jnp.exp(g_last), (8, LANES))

    def prep_phase(base, n, g_ref):
        nb = _pick_tile(n, 4)

        def body(r, carry):
            prep_group(base, r * nb, g_ref, nb)
            return carry

        lax.fori_loop(0, n // nb, body, 0)

    prep_phase(0, lc // c, gc_ref)
    prep_phase(lc, l // c, gl_ref)

    sf_scr[...] = jnp.zeros_like(sf_scr)
    sb_scr[...] = jnp.zeros_like(sb_scr)

    def two_dirs(base, chunk_ids, gate_refs):
        dirs = []
        for ci, gate_ref, s_scr, bufs in zip(chunk_ids, gate_refs, (sf_scr, sb_scr), (fwd_bufs, bwd_bufs)):
            gci = base // c + ci
            dirs.append(dict(rows=pl.ds(pl.multiple_of(ci * c, c), c), gci=gci, gate_ref=gate_ref, s_scr=s_scr,
                             bufs=bufs, srows=pl.ds(pl.multiple_of(gci * c, c), c)))
        for dr in dirs:
            dr["s"] = dr["s_scr"][...]
            wq = dr["bufs"][1][pl.ds(pl.multiple_of(dr["gci"] * 2 * c, 2 * c), 2 * c), :]
            dr["r"] = _dot(wq, dr["s"].astype(BF16))
        for dr in dirs:
            dr["v_new"] = (dr["bufs"][0][dr["srows"], :] - dr["r"][0:c, :]).astype(BF16)
        for dr in dirs:
            gl = dr["bufs"][4][pl.ds(pl.multiple_of(dr["gci"] * 8, 8), 8), :]
            kgt = dr["bufs"][2][pl.ds(pl.multiple_of(dr["gci"] * dk, dk), dk), :]
            dr["s_scr"][...] = dr["s"] * gl[0:1, :] + _dot(kgt, dr["v_new"])
        outs = []
        for dr in dirs:
            if dr["gate_ref"] is None:
                outs += [dr["rows"], None]
                continue
            o = dr["r"][c:2 * c, :] + _dot(dr["bufs"][3][dr["srows"], :], dr["v_new"])
            y = o * lax.rsqrt(jnp.mean(o * o, axis=-1, keepdims=True) + EPS) * norm_g
            outs += [dr["rows"], y * _silu(dr["gate_ref"][0, dr["rows"], :].astype(F32))]
        return outs

    def make_step(base, g_ref, gf_ref, gb_ref, y_ref):
        def step(cf, cb, second_half):
            rows_f, yf, rows_b, yb = two_dirs(base, (cf, cb), (gf_ref, gb_ref))
            if second_half is None:
                return
            if second_half:
                y_ref[0, rows_f, :] = (yacc_scr[rows_f, :] + yf).astype(y_ref.dtype)
                y_ref[0, rows_b, :] = (yacc_scr[rows_b, :] + yb).astype(y_ref.dtype)
            else:
                yacc_scr[rows_f, :] = yf
                yacc_scr[rows_b, :] = yb
        return step

    if with_ctx_out:
        _run_bidirectional(lc // c, make_step(0, gc_ref, gfc_ref, gbc_ref, yc_ref), True)
    else:
        _run_bidirectional(lc // c, make_step(0, gc_ref, None, None, None), False)
    _run_bidirectional(l // c, make_step(lc, gl_ref, gfl_ref, gbl_ref, yl_ref), True)


def _deltanet_core(p_lat, g_lat, p_ctx, g_ctx, conv_w, a_lanes, dtb_lanes, norm_g, with_ctx_out):
    b, l, _ = p_lat.shape
    lc = p_ctx.shape[1]
    nh = DN_HEADS
    dk = norm_g.shape[-1]

    def specs(n):
        sp = [pl.BlockSpec((1, n, dk), lambda bi, h, k=k: (bi, 0, k * nh + h)) for k in range(5)]
        return sp + [pl.BlockSpec((1, n, LANES), lambda bi, h: (bi, 0, 0))]

    const2 = lambda bi, h: (0, 0)
    kw = conv_w.shape[0]
    c = DN_CHUNK
    ltot = lc + l
    nck = ltot // c
    dir_bufs = [pltpu.VMEM((ltot, dk), F32), pltpu.VMEM((2 * ltot, dk), BF16), pltpu.VMEM((nck * dk, c), BF16),
                pltpu.VMEM((ltot, c), BF16), pltpu.VMEM((nck * 8, LANES), F32)]
    out_specs = [pl.BlockSpec((1, l, dk), lambda bi, h: (bi, 0, h))]
    out_shape = [jax.ShapeDtypeStruct((b, l, nh * dk), BF16)]
    if with_ctx_out:
        out_specs.append(pl.BlockSpec((1, lc, dk), lambda bi, h: (bi, 0, h)))
        out_shape.append(jax.ShapeDtypeStruct((b, lc, nh * dk), BF16))
    outs = pl.pallas_call(
        functools.partial(_dn_kernel, with_ctx_out=with_ctx_out),
        grid=(b, nh),
        in_specs=[pl.BlockSpec((1, LANES), const2), pl.BlockSpec((1, LANES), const2), pl.BlockSpec((1, dk), const2)]
                 + [pl.BlockSpec((kw, dk), lambda bi, h, k=k: (0, k * nh + h)) for k in range(3)]
                 + specs(l) + specs(lc),
        out_specs=out_specs, out_shape=out_shape,
        scratch_shapes=[pltpu.VMEM((lc + l, dk), BF16), pltpu.VMEM((lc + l, dk), BF16),
                        pltpu.VMEM((lc + l, dk), BF16), pltpu.VMEM((max(l, lc) + 16, dk), F32),
                        pltpu.VMEM((dk, dk), F32), pltpu.VMEM((dk, dk), F32),
                        pltpu.VMEM((max(l, lc), dk), F32)] + 2 * dir_bufs,
        compiler_params=_cparams(("parallel", "parallel")),
        name="deltanet_core",
    )(a_lanes, dtb_lanes, norm_g.reshape(1, dk), conv_w, conv_w, conv_w,
      *([p_lat] * 5), g_lat, *([p_ctx] * 5), g_ctx)
    return (outs[0], outs[1]) if with_ctx_out else (outs[0], None)


DN_HEADS_PER_STEP = 2


def _dn_kernel2(alog_ref, dtb_ref, ng_ref, cw_ref_q, cw_ref_k, cw_ref_v,
                ql_ref, kl_ref, vl_ref, gfl_ref, gbl_ref, gl_ref,
                qc_ref, kc_ref, vc_ref, gfc_ref, gbc_ref, gc_ref,
                yl_ref, *rest, with_ctx_out):
    if with_ctx_out:
        yc_ref, rest = rest[0], rest[1:]
    else:
        yc_ref = None
    q_scr, k_scr, v_scr, g_scr, xp_scr, sf_scr, sb_scr, yacc_scr = rest[:8]
    fwd_bufs, bwd_bufs = rest[8:13], rest[13:18]
    c = DN_CHUNK
    nh = DN_HEADS
    hp = DN_HEADS_PER_STEP
    l = ql_ref.shape[1]
    lc = qc_ref.shape[1]
    dk = ql_ref.shape[2] // hp
    hg = pl.program_id(1)
    kw = DN_CONV_W
    cpad = 8
    n_c, n_l = lc // c, l // c
    nck = n_c + n_l
    heads = [(p, slice(p * dk, (p + 1) * dk)) for p in range(hp)]

    def conv_seq(src_ref, cw_ref, dst_scr, base, n, norm, scale):
        xp_scr[0:cpad, :] = jnp.zeros((cpad, hp * dk), F32)
        xp_scr[cpad:cpad + n, :] = src_ref[0].astype(F32)
        xp_scr[cpad + n:cpad + n + cpad, :] = jnp.zeros((cpad, hp * dk), F32)
        rb = _pick_tile(n, 128)

        def body(r, carry):
            start = pl.multiple_of(r * rb, rb)
            win = xp_scr[pl.ds(start, rb + 2 * cpad), :]
            y = None
            for t in range(kw):
                o = cpad - kw // 2 + t
                term = win[o:o + rb, :] * cw_ref[t:t + 1, :]
                y = term if y is None else y + term
            y = _silu(y)
            if norm:
                y = jnp.concatenate([y[:, hs] * (lax.rsqrt(jnp.sum(y[:, hs] * y[:, hs], axis=-1, keepdims=True) + EPS)
                                                 * scale) for _, hs in heads], axis=1)
            dst_scr[pl.ds(pl.multiple_of(base + start, rb), rb), :] = y.astype(BF16)
            return carry

        lax.fori_loop(0, n // rb, body, 0)

    for src_c, src_l, cw, dst, norm, scale in ((qc_ref, ql_ref, cw_ref_q, q_scr, True, dk ** -0.5),
                                               (kc_ref, kl_ref, cw_ref_k, k_scr, True, 1.0),
                                               (vc_ref, vl_ref, cw_ref_v, v_scr, False, 1.0)):
        conv_seq(src_c, cw, dst, 0, lc, norm, scale)
        conv_seq(src_l, cw, dst, lc, l, norm, scale)
    g_scr[0:lc, :] = gc_ref[0]
    g_scr[lc:lc + l, :] = gl_ref[0]

    lane = lax.broadcasted_iota(jnp.int32, (1, LANES), 1)
    neg_a = -jnp.exp(alog_ref[...])
    dtb = dtb_ref[...]
    ii = lax.broadcasted_iota(jnp.int32, (c, c), 0)
    jj = lax.broadcasted_iota(jnp.int32, (c, c), 1)
    tri_l = (ii >= jj).astype(BF16)
    tri_u = (ii <= jj).astype(BF16)

    def lane_col(x, idx):
        return jnp.broadcast_to(jnp.sum(jnp.where(lane == idx, x, 0.0), axis=1, keepdims=True), (c, LANES))

    def prep_group(cis):
        chains = []
        for ci in cis:
            srows = pl.ds(pl.multiple_of(ci * c, c), c)
            graw = g_scr[srows, :]
            sig = jax.nn.sigmoid(graw)
            la_all = neg_a * jax.nn.softplus(graw + dtb)
            for p, hs in heads:
                h = hg * hp + p
                q, k = q_scr[srows, hs], k_scr[srows, hs]
                qf, kf, vf = q.astype(F32), k.astype(F32), v_scr[srows, hs].astype(F32)
                kk, qk = _dot_nt(k, k), _dot_nt(q, k)
                for upper, bufs in ((False, fwd_bufs), (True, bwd_bufs)):
                    beta = lane_col(sig, h + (nh if upper else 0))
                    la = lane_col(la_all, h + (3 * nh if upper else 2 * nh))
                    chains.append(dict(upper=upper, bufs=bufs, ci=ci, p=p, hs=hs, srows=srows, qf=qf, kf=kf, vf=vf,
                                       kk=kk, qk=qk, beta=beta, la=la))
        for ch in chains:
            ch["g"] = _cumsum_rows(tri_u if ch["upper"] else tri_l, ch["la"])
        for ch in chains:
            g = ch["g"]
            keep = (ii <= jj) if ch["upper"] else (ii >= jj)
            g_row = g.T[0:c, 0:c]
            ch["decay"] = jnp.where(keep, jnp.exp(jnp.where(keep, g[:, 0:c] - g_row, 0.0)), 0.0)
        t_invs = _unit_tri_inverses([jnp.where(ii != jj, ch["kk"] * ch["beta"][:, 0:c] * ch["decay"], 0.0)
                                     for ch in chains])
        sols = []
        for ch, t_inv in zip(chains, t_invs):
            ch["eg"] = jnp.exp(ch["g"])
            rhs = jnp.concatenate([ch["vf"] * ch["beta"], ch["kf"] * (ch["beta"] * ch["eg"])], axis=1).astype(BF16)
            sols.append(_dot(t_inv.astype(BF16), rhs))
        for ch, sol in zip(chains, sols):
            u_scr, wq_scr, kgt_scr, at_scr, gl_scr = ch["bufs"]
            g, ci, p, hs, srows = ch["g"], ch["ci"], ch["p"], ch["hs"], ch["srows"]
            g_last = g[0:1, :] if ch["upper"] else g[c - 1:c, :]
            u_scr[srows, hs] = sol[:, 0:dk]
            wq_scr[pl.ds(pl.multiple_of(ci * 2 * c, c), c), hs] = sol[:, dk:2 * dk].astype(BF16)
            wq_scr[pl.ds(pl.multiple_of(ci * 2 * c + c, c), c), hs] = (ch["qf"] * ch["eg"]).astype(BF16)
            kgt_scr[p, pl.ds(pl.multiple_of(ci * dk, dk), dk), :] = (ch["kf"] * jnp.exp(g_last - g)).T.astype(BF16)
            at_scr[p, srows, :] = (ch["qk"] * ch["decay"]).astype(BF16)
            gl_scr[p, pl.ds(pl.multiple_of(ci * 8, 8), 8), :] = jnp.broadcast_to(jnp.exp(g_last), (8, LANES))

    nb = next(t for t in (6, 4, 2, 1) if nck % t == 0)

    def prep_body(r, carry):
        prep_group([r * nb + t for t in range(nb)])
        return carry

    lax.fori_loop(0, nck // nb, prep_body, 0)

    sf_scr[...] = jnp.zeros_like(sf_scr)
    sb_scr[...] = jnp.zeros_like(sb_scr)
    norm_g = ng_ref[...]

    def step_all(base, chunk_ids, gate_refs):
        chains = []
        for ci, gate_ref, s_scr, bufs in zip(chunk_ids, gate_refs, (sf_scr, sb_scr), (fwd_bufs, bwd_bufs)):
            gci = base // c + ci
            for p, hs in heads:
                chains.append(dict(rows=pl.ds(pl.multiple_of(ci * c, c), c), gci=gci, gate_ref=gate_ref, s_scr=s_scr,
                                   bufs=bufs, p=p, hs=hs, srows=pl.ds(pl.multiple_of(gci * c, c), c)))
        for ch in chains:
            ch["s"] = ch["s_scr"][ch["p"]]
            wq = ch["bufs"][1][pl.ds(pl.multiple_of(ch["gci"] * 2 * c, 2 * c), 2 * c), ch["hs"]]
            ch["r"] = _dot(wq, ch["s"].astype(BF16))
        for ch in chains:
            ch["v_new"] = (ch["bufs"][0][ch["srows"], ch["hs"]] - ch["r"][0:c, :]).astype(BF16)
        for ch in chains:
            gl = ch["bufs"][4][ch["p"], pl.ds(pl.multiple_of(ch["gci"] * 8, 8), 8), :]
            kgt = ch["bufs"][2][ch["p"], pl.ds(pl.multiple_of(ch["gci"] * dk, dk), dk), :]
            ch["s_scr"][ch["p"]] = ch["s"] * gl[0:1, :] + _dot(kgt, ch["v_new"])
        outs = []
        for ch in chains:
            if ch["gate_ref"] is None:
                outs.append(None)
                continue
            o = ch["r"][c:2 * c, :] + _dot(ch["bufs"][3][ch["p"], ch["srows"], :], ch["v_new"])
            y = o * lax.rsqrt(jnp.mean(o * o, axis=-1, keepdims=True) + EPS) * norm_g
            outs.append(y * _silu(ch["gate_ref"][0, ch["rows"], ch["hs"]].astype(F32)))
        return chains, outs

    def make_step(base, gf_ref, gb_ref, y_ref):
        def step(cf, cb, second_half):
            chains, outs = step_all(base, (cf, cb), (gf_ref, gb_ref))
            if second_half is None:
                return
            for ch, y in zip(chains, outs):
                if second_half:
                    y_ref[0, ch["rows"], ch["hs"]] = (yacc_scr[ch["rows"], ch["hs"]] + y).astype(y_ref.dtype)
                else:
                    yacc_scr[ch["rows"], ch["hs"]] = y
        return step

    if with_ctx_out:
        _run_bidirectional(n_c, make_step(0, gfc_ref, gbc_ref, yc_ref), True)
    else:
        _run_bidirectional(n_c, make_step(0, None, None, None), False)
    _run_bidirectional(n_l, make_step(lc, gfl_ref, gbl_ref, yl_ref), True)


def _deltanet_core2(p_lat, g_lat, p_ctx, g_ctx, conv_w, a_lanes, dtb_lanes, norm_g, with_ctx_out):
    b, l, _ = p_lat.shape
    lc = p_ctx.shape[1]
    nh = DN_HEADS
    hp = DN_HEADS_PER_STEP
    ng = nh // hp
    dk = norm_g.shape[-1]
    w = hp * dk

    def specs(n):
        sp = [pl.BlockSpec((1, n, w), lambda bi, h, k=k: (bi, 0, k * ng + h)) for k in range(5)]
        return sp + [pl.BlockSpec((1, n, LANES), lambda bi, h: (bi, 0, 0))]

    const2 = lambda bi, h: (0, 0)
    kw = conv_w.shape[0]
    c = DN_CHUNK
    ltot = lc + l
    nck = ltot // c
    dir_bufs = [pltpu.VMEM((ltot, w), F32), pltpu.VMEM((2 * ltot, w), BF16), pltpu.VMEM((hp, nck * dk, c), BF16),
                pltpu.VMEM((hp, ltot, c), BF16), pltpu.VMEM((hp, nck * 8, LANES), F32)]
    out_specs = [pl.BlockSpec((1, l, w), lambda bi, h: (bi, 0, h))]
    out_shape = [jax.ShapeDtypeStruct((b, l, nh * dk), BF16)]
    if with_ctx_out:
        out_specs.append(pl.BlockSpec((1, lc, w), lambda bi, h: (bi, 0, h)))
        out_shape.append(jax.ShapeDtypeStruct((b, lc, nh * dk), BF16))
    outs = pl.pallas_call(
        functools.partial(_dn_kernel2, with_ctx_out=with_ctx_out),
        grid=(b, ng),
        in_specs=[pl.BlockSpec((1, LANES), const2), pl.BlockSpec((1, LANES), const2), pl.BlockSpec((1, dk), const2)]
                 + [pl.BlockSpec((kw, w), lambda bi, h, k=k: (0, k * ng + h)) for k in range(3)]
                 + specs(l) + specs(lc),
        out_specs=out_specs, out_shape=out_shape,
        scratch_shapes=[pltpu.VMEM((ltot, w), BF16), pltpu.VMEM((ltot, w), BF16), pltpu.VMEM((ltot, w), BF16),
                        pltpu.VMEM((ltot, LANES), F32), pltpu.VMEM((max(l, lc) + 16, w), F32),
                        pltpu.VMEM((hp, dk, dk), F32), pltpu.VMEM((hp, dk, dk), F32),
                        pltpu.VMEM((max(l, lc), w), F32)] + 2 * dir_bufs,
        compiler_params=pltpu.CompilerParams(dimension_semantics=("parallel", "parallel"),
                                             vmem_limit_bytes=DN_VMEM_LIMIT),
        name="deltanet_core",
    )(a_lanes, dtb_lanes, norm_g.reshape(1, dk), conv_w, conv_w, conv_w,
      *([p_lat] * 5), g_lat, *([p_ctx] * 5), g_ctx)
    return (outs[0], outs[1]) if with_ctx_out else (outs[0], None)


def kernel(x, c, ctx, c_ctx, ada_w, ada_b, mix_pre_g, mix_post_g, mlp_pre_g, mlp_post_g, mlp_w1, mlp_w2, ret_w_in, ret_decay_logit, ret_w_out, att_w_in, att_sink, att_w_out, pool_w, pool_b, pool_scale, dn_w_in, dn_conv_w, dn_a_log, dn_dt_bias, dn_norm_g, dn_w_out):
    b, l, d = x.shape
    lc = ctx.shape[1]
    depth = ada_w.shape[0]
    n_mixers = 4

    rows = -(-(b + 1) // 8) * 8
    cond = jnp.zeros((rows, d), F32).at[:b].set(c).at[b].set(c_ctx)
    mods = _modulation(cond, ada_w, ada_b)
    xc = ctx.reshape(1, b * lc, d)

    for i in range(depth):
        kind, inst = i % n_mixers, i // n_mixers
        need_ctx = i < depth - 1
        m_lat = mods[i, :b].reshape(b, N_MOD, d)
        m_ctx = mods[i, b:b + 1].reshape(1, N_MOD, d)
        w1, w2 = mlp_w1[i].astype(BF16), mlp_w2[i].astype(BF16)
        post_kw = {}
        if kind == 0:
            w_in = ret_w_in[inst].astype(BF16)
            p_lat = _inproj(x, m_lat, mix_pre_g[i], w_in, name="ret_inproj")
            p_ctx = _inproj(xc, m_ctx, mix_pre_g[i], w_in, name="ret_inproj_ctx").reshape(b, lc, -1)
            dk = d // RET_HEADS
            cos, sin = _rope_tables(l, dk, 1)
            a_lat, a_ctx = _retention_core(p_lat, p_ctx, ret_decay_logit[inst], cos, sin)
            w_out = ret_w_out[inst].astype(BF16)
        elif kind == 1:
            grp = ATT_HEADS // ATT_KV_HEADS
            dh = d // ATT_HEADS
            nq, nkv = ATT_HEADS * dh, ATT_KV_HEADS * dh
            wq = att_w_in[inst][:, :nq].reshape(d, ATT_KV_HEADS, grp * dh)
            wk = att_w_in[inst][:, nq:nq + nkv].reshape(d, ATT_KV_HEADS, dh)
            wv = att_w_in[inst][:, nq + nkv:].reshape(d, ATT_KV_HEADS, dh)
            w_in = jnp.concatenate([wq, wk, wv], axis=2).reshape(d, nq + 2 * nkv).astype(BF16)
            p_lat = _inproj(x, m_lat, mix_pre_g[i], w_in, name="att_inproj")
            p_ctx = _inproj(xc, m_ctx, mix_pre_g[i], w_in, name="att_inproj_ctx").reshape(b, lc, -1)
            cos_h, sin_h = _rope_tables(l, dh, 1)
            one, zero = jnp.ones((l, dh), F32), jnp.zeros((l, dh), F32)
            qs = dh ** -0.5
            cos = jnp.concatenate([jnp.tile(cos_h, (1, grp)) * qs, cos_h, one], axis=1)
            sin = jnp.concatenate([jnp.tile(sin_h, (1, grp)) * qs, sin_h, zero], axis=1)
            a_lat, a_ctx = _attention_core(p_lat, p_ctx, att_sink[inst], cos, sin, need_ctx)
            w_out = att_w_out[inst].astype(BF16)
        elif kind == 2:
            a_lat = _pool_core(x, m_lat, mix_pre_g[i])
            a_ctx = _pool_core(xc.reshape(b, lc, d), jnp.broadcast_to(m_ctx, (b, N_MOD, d)), mix_pre_g[i]) if need_ctx else None
            w_out = pool_w[inst].astype(BF16)
            post_kw = dict(pool_bias=pool_b[inst], pool_scale=pool_scale[inst])
        else:
            nh = DN_HEADS
            dk = d // nh
            nqkv = 3 * nh * dk
            wd = dn_w_in[inst]
            w_main = jnp.concatenate([wd[:, :nqkv], wd[:, nqkv + 4 * nh:]], axis=1).astype(BF16)
            w_gate = jnp.zeros((d, LANES), F32).at[:, :4 * nh].set(wd[:, nqkv:nqkv + 4 * nh]).astype(BF16)
            p_lat, g_lat = _inproj(x, m_lat, mix_pre_g[i], w_main, aux_w=w_gate, name="dn_inproj")
            p_ctx, g_ctx = _inproj(xc, m_ctx, mix_pre_g[i], w_main, aux_w=w_gate, name="dn_inproj_ctx")
            p_ctx, g_ctx = p_ctx.reshape(b, lc, -1), g_ctx.reshape(b, lc, -1)
            a_lanes = jnp.zeros((1, LANES), F32).at[0, 2 * nh:4 * nh].set(dn_a_log[inst].reshape(-1))
            dtb_lanes = jnp.zeros((1, LANES), F32).at[0, 2 * nh:4 * nh].set(dn_dt_bias[inst].reshape(-1))
            a_lat, a_ctx = _deltanet_core2(p_lat, g_lat, p_ctx, g_ctx, dn_conv_w[inst], a_lanes, dtb_lanes,
                                          dn_norm_g[inst], need_ctx)
            w_out = dn_w_out[inst].astype(BF16)

        x = _post(a_lat, x, m_lat, mix_post_g[i], mlp_pre_g[i], mlp_post_g[i], w_out, w1, w2,
                  name="post", **post_kw)
        if need_ctx:
            xc = _post(a_ctx.reshape(1, b * lc, -1), xc, m_ctx, mix_post_g[i], mlp_pre_g[i], mlp_post_g[i],
                       w_out, w1, w2, name="post_ctx", **post_kw)
    return x
```

```python
import functools

import jax
import jax.numpy as jnp
from jax import lax
from jax.experimental import pallas as pl
from jax.experimental.pallas import tpu as pltpu

F32 = jnp.float32
BF16 = jnp.bfloat16
HI = lax.Precision.HIGHEST

EPS = 1e-6
NEG_INF = -1e30
ROPE_BASE = 10000.0
GRID_W = 64
N_MOD = 6
RET_HEADS = 8
RET_CHUNK = 128
ATT_HEADS = 16
ATT_KV_HEADS = 4
ATT_WINDOW = 128
ATT_BLOCK = 128
POOL_WINDOWS = (2, 4, 8, 16)
POOL_PAD = 16
DN_HEADS = 8
DN_CHUNK = 64
DN_CONV_W = 5

LANES = 128
VMEM_LIMIT = 48 * 1024 * 1024
DN_VMEM_LIMIT = 56 * 1024 * 1024
INPROJ_MAX_TN = 2560


def _cparams(sem):
    return pltpu.CompilerParams(dimension_semantics=sem, vmem_limit_bytes=VMEM_LIMIT)


def _dot(a, b):
    return jnp.dot(a, b, preferred_element_type=F32)


def _dot_nt(a, b):
    return lax.dot_general(a, b, (((1,), (1,)), ((), ())), preferred_element_type=F32)


def _dot_hi(a, b):
    return jnp.dot(a, b, preferred_element_type=F32, precision=HI)


def _rms(x):
    return x * lax.rsqrt(jnp.mean(x * x, axis=-1, keepdims=True) + EPS)


def _silu(x):
    return x * jax.nn.sigmoid(x)


def _pick_tile(n, pref):
    t = min(pref, n)
    while n % t:
        t //= 2
    return t


def _mod_kernel(c_ref, w_ref, b_ref, o_ref):
    o_ref[0] = _dot_hi(_silu(c_ref[...]), w_ref[0]) + b_ref[0]


def _modulation(cond, ada_w, ada_b):
    depth, d, n = ada_w.shape
    rows = cond.shape[0]
    tn = _pick_tile(n, 1536)
    return pl.pallas_call(
        _mod_kernel,
        grid=(depth, n // tn),
        in_specs=[pl.BlockSpec((rows, d), lambda i, j: (0, 0)),
                  pl.BlockSpec((1, d, tn), lambda i, j: (i, 0, j)),
                  pl.BlockSpec((1, 1, tn), lambda i, j: (i, 0, j))],
        out_specs=pl.BlockSpec((1, rows, tn), lambda i, j: (i, 0, j)),
        out_shape=jax.ShapeDtypeStruct((depth, rows, n), F32),
        compiler_params=_cparams(("parallel", "parallel")),
        name="modulation",
    )(cond, ada_w, ada_b.reshape(depth, 1, n))


def _inproj_kernel(x_ref, mod_ref, g_ref, w_ref, *rest, has_aux):
    if has_aux:
        wa_ref, o_ref, oa_ref, u_scr = rest
    else:
        o_ref, u_scr = rest

    @pl.when(pl.program_id(2) == 0)
    def _():
        m = mod_ref[0]
        u = _rms(x_ref[0]) * g_ref[...] * (1.0 + m[1:2]) + m[0:1]
        u_scr[...] = u.astype(BF16)
        if has_aux:
            oa_ref[0] = _dot(u_scr[...], wa_ref[...])

    o_ref[0] = _dot(u_scr[...], w_ref[...]).astype(o_ref.dtype)


def _inproj(x, mod, g, w, aux_w=None, name="inproj"):
    bx, lx, d = x.shape
    n = w.shape[1]
    tm = _pick_tile(lx, 1024)
    tn = max(t for t in range(LANES, min(n, INPROJ_MAX_TN) + 1, LANES) if n % t == 0)
    has_aux = aux_w is not None
    in_specs = [pl.BlockSpec((1, tm, d), lambda b, i, j: (b, i, 0)),
                pl.BlockSpec((1, N_MOD, d), lambda b, i, j: (b, 0, 0)),
                pl.BlockSpec((1, d), lambda b, i, j: (0, 0)),
                pl.BlockSpec((d, tn), lambda b, i, j: (0, j))]
    out_specs = [pl.BlockSpec((1, tm, tn), lambda b, i, j: (b, i, j))]
    out_shape = [jax.ShapeDtypeStruct((bx, lx, n), BF16)]
    args = [x, mod, g.reshape(1, d), w]
    if has_aux:
        na = aux_w.shape[1]
        in_specs.append(pl.BlockSpec((d, na), lambda b, i, j: (0, 0)))
        out_specs.append(pl.BlockSpec((1, tm, na), lambda b, i, j: (b, i, 0)))
        out_shape.append(jax.ShapeDtypeStruct((bx, lx, na), F32))
        args.append(aux_w)
    outs = pl.pallas_call(
        functools.partial(_inproj_kernel, has_aux=has_aux),
        grid=(bx, lx // tm, n // tn),
        in_specs=in_specs, out_specs=out_specs, out_shape=out_shape,
        scratch_shapes=[pltpu.VMEM((tm, d), BF16)],
        compiler_params=_cparams(("parallel", "parallel", "arbitrary")),
        name=name,
    )(*args)
    return outs if has_aux else outs[0]


def _post_kernel(a_ref, x_ref, mod_ref, gpost_ref, gpre2_ref, gpost2_ref, wo_ref, *rest, grouped, tf):
    if grouped:
        pb_ref, ps_ref, w1_ref, w2_ref, o_ref = rest
    else:
        w1_ref, w2_ref, o_ref = rest
    a = a_ref[0]
    if grouped:
        ng, gw, _ = wo_ref.shape
        y = jnp.concatenate([_dot(a[:, g * gw:(g + 1) * gw], wo_ref[g]) for g in range(ng)], axis=1)
        y = (y + pb_ref[...]) * ps_ref[...]
    else:
        y = _dot(a, wo_ref[...])
    m = mod_ref[0]
    x1 = x_ref[0] + m[2:3] * (_rms(y) * gpost_ref[...])
    u = (_rms(x1) * gpre2_ref[...] * (1.0 + m[4:5]) + m[3:4]).astype(BF16)
    acc = None
    for j in range(w1_ref.shape[1] // tf):
        h = jnp.square(jnp.maximum(_dot(u, w1_ref[:, j * tf:(j + 1) * tf]), 0.0)).astype(BF16)
        t = _dot(h, w2_ref[j * tf:(j + 1) * tf, :])
        acc = t if acc is None else acc + t
    o_ref[0] = x1 + m[5:6] * (_rms(acc) * gpost2_ref[...])


def _post(a, x, mod, gpost, gpre2, gpost2, w_out, w1, w2, pool_bias=None, pool_scale=None, name="post"):
    bx, lx, d = x.shape
    ka = a.shape[2]
    dff = w1.shape[1]
    tm = _pick_tile(lx, 512)
    tf = _pick_tile(dff, 512)
    grouped = pool_bias is not None
    row = lambda v: v.reshape(1, d)
    const2 = lambda b, i: (0, 0)
    once = pl.Buffered(1)
    in_specs = [pl.BlockSpec((1, tm, ka), lambda b, i: (b, i, 0)),
                pl.BlockSpec((1, tm, d), lambda b, i: (b, i, 0)),
                pl.BlockSpec((1, N_MOD, d), lambda b, i: (b, 0, 0)),
                pl.BlockSpec((1, d), const2), pl.BlockSpec((1, d), const2), pl.BlockSpec((1, d), const2)]
    args = [a, x, mod, row(gpost), row(gpre2), row(gpost2), w_out]
    if grouped:
        in_specs += [pl.BlockSpec(w_out.shape, lambda b, i: (0, 0, 0), pipeline_mode=once),
                     pl.BlockSpec((1, d), const2), pl.BlockSpec((1, d), const2)]
        args += [row(pool_bias), row(pool_scale)]
    else:
        in_specs += [pl.BlockSpec(w_out.shape, const2, pipeline_mode=once)]
    in_specs += [pl.BlockSpec((d, dff), const2, pipeline_mode=once), pl.BlockSpec((dff, d), const2, pipeline_mode=once)]
    args += [w1, w2]
    return pl.pallas_call(
        functools.partial(_post_kernel, grouped=grouped, tf=tf),
        grid=(bx, lx // tm),
        in_specs=in_specs,
        out_specs=pl.BlockSpec((1, tm, d), lambda b, i: (b, i, 0)),
        out_shape=jax.ShapeDtypeStruct((bx, lx, d), F32),
        compiler_params=_cparams(("parallel", "parallel")),
        name=name,
    )(*args)


def _rope_tables(n, head_dim, n_rep, scale=1.0):
    quarter = head_dim // 4
    inv = ROPE_BASE ** (-jnp.arange(quarter, dtype=F32) / quarter)
    t = jnp.arange(n, dtype=jnp.int32)
    ang_r = (t // GRID_W).astype(F32)[:, None] * inv[None, :]
    ang_c = (t % GRID_W).astype(F32)[:, None] * inv[None, :]
    cos = jnp.concatenate([jnp.cos(ang_r), jnp.cos(ang_r), jnp.cos(ang_c), jnp.cos(ang_c)], axis=1)
    sin = jnp.concatenate([-jnp.sin(ang_r), jnp.sin(ang_r), -jnp.sin(ang_c), jnp.sin(ang_c)], axis=1)
    return jnp.tile(cos, (1, n_rep)) * scale, jnp.tile(sin, (1, n_rep)) * scale


def _rope(x, cos, sin, quarter):
    r = lax.broadcasted_iota(jnp.int32, (LANES, LANES), 0)
    col = lax.broadcasted_iota(jnp.int32, (LANES, LANES), 1)
    partner = jnp.where((col & (2 * quarter - 1)) < quarter, col + quarter, col - quarter)
    perm = (r == partner).astype(BF16)
    swapped = jnp.concatenate([_dot(x[:, g * LANES:(g + 1) * LANES], perm) for g in range(x.shape[1] // LANES)], axis=1)
    return x.astype(F32) * cos + swapped * sin


def _head_scalar(vec, h):
    lane = lax.broadcasted_iota(jnp.int32, vec.shape, 1)
    return jnp.sum(jnp.where(lane == h, vec, 0.0), axis=1, keepdims=True)


def _log_sigmoid(x):
    return jnp.minimum(x, 0.0) - jnp.log1p(jnp.exp(-jnp.abs(x)))


def _run_bidirectional(n, step, want_out):
    unroll = 2 if n % 4 == 0 else 1
    if want_out:
        assert n % 2 == 0
        lax.fori_loop(0, n // 2, lambda i, c: step(i, n - 1 - i, False) or c, 0, unroll=unroll)
        lax.fori_loop(n // 2, n, lambda i, c: step(i, n - 1 - i, True) or c, 0, unroll=unroll)
    else:
        lax.fori_loop(0, n, lambda i, c: step(i, n - 1 - i, None) or c, 0, unroll=unroll)


def _ret_kernel(dl_ref, cos_ref, sin_ref, ql_ref, kl_ref, vl_ref, gfl_ref, gbl_ref,
                qc_ref, kc_ref, vc_ref, gfc_ref, gbc_ref, yl_ref, yc_ref,
                q_scr, k_scr, v_scr, p_scr, s_all_scr):
    c = RET_CHUNK
    l = ql_ref.shape[1]
    lc = qc_ref.shape[1]
    dk = ql_ref.shape[2]
    h = pl.program_id(1)
    scale = dk ** -0.5

    ls = _log_sigmoid(dl_ref[...])
    lgf = _head_scalar(ls[0:1], h)
    lgb = _head_scalar(ls[1:2], h)

    q_scr[0:lc, :] = (qc_ref[0].astype(F32) * scale).astype(BF16)
    k_scr[0:lc, :] = kc_ref[0]
    rb = _pick_tile(l, 256)

    def rope_body(r, carry):
        rows = pl.ds(pl.multiple_of(r * rb, rb), rb)
        dst = pl.ds(pl.multiple_of(lc + r * rb, rb), rb)
        cs, sn = cos_ref[rows, :], sin_ref[rows, :]
        q_scr[dst, :] = (_rope(ql_ref[0, rows, :], cs, sn, dk // 4) * scale).astype(BF16)
        k_scr[dst, :] = _rope(kl_ref[0, rows, :], cs, sn, dk // 4).astype(BF16)
        return carry

    lax.fori_loop(0, l // rb, rope_body, 0, unroll=2 - (l // rb) % 2)

    ii = lax.broadcasted_iota(jnp.int32, (c, c), 0).astype(F32)
    jj = lax.broadcasted_iota(jnp.int32, (c, c), 1).astype(F32)
    diff = ii - jj
    dec_f = jnp.where(diff >= 0, jnp.exp(lgf * jnp.maximum(diff, 0.0)), 0.0)
    dec_b = jnp.where(diff <= 0, jnp.exp(lgb * jnp.maximum(-diff, 0.0)), 0.0)
    idx = lax.broadcasted_iota(jnp.int32, (c, 1), 0).astype(F32)
    dv = vl_ref.shape[2]
    qdec_f, cd_f = jnp.exp(lgf * (idx + 1.0)), jnp.exp(lgf * c)
    qdec_b, cd_b = jnp.exp(lgb * (c - idx)), jnp.exp(lgb * c)

    n_c, n_l = lc // c, l // c
    n = n_c + n_l
    v_scr[0:lc, :] = vc_ref[0]
    v_scr[lc:lc + l, :] = vl_ref[0]

    lane_idx = lax.broadcasted_iota(jnp.int32, (1, c), 1).astype(F32)
    kdec_f_row, kdec_b_row = jnp.exp(lgf * (c - 1.0 - lane_idx)), jnp.exp(lgb * lane_idx)

    grp = _pick_tile(n_l, 4)

    def incr_group(cis):
        srows = [pl.ds(pl.multiple_of(ci * c, c), c) for ci in cis]
        kts = [k_scr[sr, :].astype(F32).T for sr in srows]
        lhs = [jnp.concatenate([kt * kdec_f_row, kt * kdec_b_row], axis=0).astype(BF16) for kt in kts]
        ps = [_dot(a, v_scr[sr, :]) for a, sr in zip(lhs, srows)]
        for ci, p in zip(cis, ps):
            p_scr[pl.ds(pl.multiple_of(ci * 2 * dk, 2 * dk), 2 * dk), :] = p

    def incr_body(r, carry):
        incr_group([n_c + r * grp + t for t in range(grp)])
        return carry

    incr_group(list(range(n_c)))
    lax.fori_loop(0, n_l // grp, incr_body, 0)

    def scan(first, count, step, s, half, cd):
        def body(t, s):
            ci = first + t * step
            s_all_scr[pl.ds(pl.multiple_of(ci * dk, dk), dk), half * dv:(half + 1) * dv] = s.astype(BF16)
            return s * cd + p_scr[pl.ds(pl.multiple_of((ci * 2 + half) * dk, dk), dk), :]
        return lax.fori_loop(0, count, body, s)

    zero = jnp.zeros((dk, dv), F32)
    scan(0, n, 1, zero, 0, cd_f)
    scan(n - 1, n_l, -1, scan(n_c - 1, n_c, -1, zero, 1, cd_b), 1, cd_b)

    def out_group(base_chunk, cis, gf_ref, gb_ref, y_ref):
        srows = [pl.ds(pl.multiple_of((base_chunk + ci) * c, c), c) for ci in cis]
        qs = [q_scr[sr, :] for sr in srows]
        ss = [_dot_nt(q, k_scr[sr, :]) for q, sr in zip(qs, srows)]
        cross = [_dot(q, s_all_scr[pl.ds(pl.multiple_of((base_chunk + ci) * dk, dk), dk), :]) for q, ci in zip(qs, cis)]
        intra = [_dot(jnp.concatenate([s * dec_f, s * dec_b], axis=0).astype(BF16), v_scr[sr, :])
                 for s, sr in zip(ss, srows)]
        for ci, it, cr in zip(cis, intra, cross):
            rows = pl.ds(pl.multiple_of(ci * c, c), c)
            y = None
            for half, (qdec, g_ref) in enumerate(((qdec_f, gf_ref), (qdec_b, gb_ref))):
                o = it[half * c:(half + 1) * c, :] + qdec * cr[:, half * dv:(half + 1) * dv]
                t = o * lax.rsqrt(jnp.mean(o * o, axis=-1, keepdims=True) + EPS) * _silu(g_ref[0, rows, :].astype(F32))
                y = t if y is None else y + t
            y_ref[0, rows, :] = y.astype(y_ref.dtype)

    out_group(0, list(range(n_c)), gfc_ref, gbc_ref, yc_ref)

    def out_body(r, carry):
        out_group(n_c, [r * grp + t for t in range(grp)], gfl_ref, gbl_ref, yl_ref)
        return carry

    lax.fori_loop(0, n_l // grp, out_body, 0)


def _retention_core(p_lat, p_ctx, decay_logit, cos, sin):
    b, l, _ = p_lat.shape
    lc = p_ctx.shape[1]
    nh = RET_HEADS
    dk = cos.shape[1]
    dv = 2 * dk

    def specs(n):
        return [pl.BlockSpec((1, n, dk), lambda bi, h: (bi, 0, h)),
                pl.BlockSpec((1, n, dk), lambda bi, h: (bi, 0, nh + h)),
                pl.BlockSpec((1, n, dv), lambda bi, h: (bi, 0, nh + h)),
                pl.BlockSpec((1, n, dv), lambda bi, h: (bi, 0, 2 * nh + h)),
                pl.BlockSpec((1, n, dv), lambda bi, h: (bi, 0, 3 * nh + h))]

    const2 = lambda bi, h: (0, 0)
    return pl.pallas_call(
        _ret_kernel,
        grid=(b, nh),
        in_specs=[pl.BlockSpec(decay_logit.shape, const2), pl.BlockSpec((l, dk), const2),
                  pl.BlockSpec((l, dk), const2)] + specs(l) + specs(lc),
        out_specs=[pl.BlockSpec((1, l, dv), lambda bi, h: (bi, 0, h)),
                   pl.BlockSpec((1, lc, dv), lambda bi, h: (bi, 0, h))],
        out_shape=[jax.ShapeDtypeStruct((b, l, nh * dv), BF16), jax.ShapeDtypeStruct((b, lc, nh * dv), BF16)],
        scratch_shapes=[pltpu.VMEM((lc + l, dk), BF16), pltpu.VMEM((lc + l, dk), BF16),
                        pltpu.VMEM((lc + l, dv), BF16),
                        pltpu.VMEM(((lc + l) // RET_CHUNK * 2 * dk, dv), F32),
                        pltpu.VMEM(((lc + l) // RET_CHUNK * dk, 2 * dv), BF16)],
        compiler_params=_cparams(("parallel", "parallel")),
        name="retention_core",
    )(decay_logit, cos, sin, *([p_lat] * 5), *([p_ctx] * 5))


def _att_kernel(sink_ref, cos_ref, sin_ref, pl_ref, pc_ref, ol_ref, *rest, with_ctx_out):
    if with_ctx_out:
        oc_ref, rest = rest[0], rest[1:]
    q_scr, k_scr, v_scr, kc_scr, vc_scr, bias_scr = rest
    grp = ATT_HEADS // ATT_KV_HEADS
    qb, win = ATT_BLOCK, ATT_WINDOW
    span = qb + 2 * win
    l = pl_ref.shape[1]
    lc = pc_ref.shape[1]
    dh = pl_ref.shape[2] // (grp + 2)
    nq = grp * dh
    nblk = l // qb
    npair = nq // LANES
    kh = pl.program_id(1)
    assert 2 * dh == LANES

    def split_kv(kv, k_dst, v_dst, rows):
        lo = lax.broadcasted_iota(jnp.int32, kv.shape, 1) < dh
        vk = pltpu.roll(kv, dh, 1)
        one = jnp.ones_like(kv)
        k_dst[0, rows, :] = jnp.where(lo, kv, 0.0).astype(BF16)
        k_dst[1, rows, :] = jnp.where(lo, 0.0, vk).astype(BF16)
        v_dst[0, rows, :] = jnp.concatenate([jnp.where(lo, vk, 0.0), jnp.where(lo, one, 0.0)], axis=1).astype(BF16)
        v_dst[1, rows, :] = jnp.concatenate([jnp.where(lo, 0.0, kv), jnp.where(lo, 0.0, one)], axis=1).astype(BF16)

    rb = _pick_tile(l, 256)

    def rope_body(r, carry):
        rows = pl.ds(pl.multiple_of(r * rb, rb), rb)
        x = _rope(pl_ref[0, rows, :], cos_ref[rows, :], sin_ref[rows, :], dh // 4)
        q_scr[rows, :] = x[:, 0:nq].astype(BF16)
        split_kv(x[:, nq:nq + 2 * dh], k_scr, v_scr, rows)
        return carry

    lax.fori_loop(0, l // rb, rope_body, 0, unroll=2 - (l // rb) % 2)
    split_kv(pc_ref[0, :, nq:nq + 2 * dh].astype(F32), kc_scr, vc_scr, slice(None))

    qi = lax.broadcasted_iota(jnp.int32, (qb, span), 0)
    kj = lax.broadcasted_iota(jnp.int32, (qb, span), 1)
    for e in range(3):
        bias_scr[e] = jnp.where(jnp.abs(kj - e * win - qi) <= win, 0.0, NEG_INF)
    lo_half = lax.broadcasted_iota(jnp.int32, (1, LANES), 1) < dh

    def attend(problems):
        scores = []
        for q, _, parts in problems:
            scores.append([[_dot_nt(q, kk[par]) if bias is None else _dot_nt(q, kk[par]) + bias
                            for kk, _, bias in parts] for par in range(2)])
        mxs = []
        for (_, pair, _), sc in zip(problems, scores):
            row = []
            for par in range(2):
                mx = jnp.full((sc[par][0].shape[0], 1), sink_ref[kh * grp + 2 * pair + par], F32)
                for s in sc[par]:
                    mx = jnp.maximum(mx, jnp.max(s, axis=-1, keepdims=True))
                row.append(mx)
            mxs.append(row)
        outs = []
        for (_, pair, parts), sc, mx in zip(problems, scores, mxs):
            acc = None
            for par in range(2):
                for s, (_, vv, _) in zip(sc[par], parts):
                    t = _dot(jnp.exp(s - mx[par]).astype(BF16), vv[par])
                    acc = t if acc is None else acc + t
            sink_e = jnp.exp(sink_ref[kh * grp + 2 * pair] - mx[0])
            sink_o = jnp.exp(sink_ref[kh * grp + 2 * pair + 1] - mx[1])
            outs.append(acc[:, 0:LANES] / (acc[:, LANES:2 * LANES] + jnp.where(lo_half, sink_e, sink_o)))
        return outs

    kc_parts = ([kc_scr[0], kc_scr[1]], [vc_scr[0], vc_scr[1]])
    nb = 2 - nblk % 2

    def block_group(r, carry):
        problems, dests = [], []
        for t in range(nb):
            bi = r * nb + t
            qs = pl.multiple_of(bi * qb, qb)
            ks = pl.multiple_of(jnp.clip(qs - win, 0, l - span), qb)
            edge = jnp.where(bi == 0, 0, jnp.where(bi == nblk - 1, 2, 1))
            bias = bias_scr[edge]
            kwin = [k_scr[par, pl.ds(ks, span), :] for par in range(2)]
            vwin = [v_scr[par, pl.ds(ks, span), :] for par in range(2)]
            for pair in range(npair):
                q = q_scr[pl.ds(qs, qb), pair * LANES:(pair + 1) * LANES]
                problems.append((q, pair, [(kwin, vwin, bias), (kc_parts[0], kc_parts[1], None)]))
                dests.append((qs, pair))
        for (qs, pair), o in zip(dests, attend(problems)):
            ol_ref[0, pl.ds(qs, qb), pair * LANES:(pair + 1) * LANES] = o.astype(ol_ref.dtype)
        return carry

    lax.fori_loop(0, nblk // nb, block_group, 0)

    if with_ctx_out:
        problems = []
        for pair in range(npair):
            qc = (pc_ref[0, :, pair * LANES:(pair + 1) * LANES].astype(F32) * dh ** -0.5).astype(BF16)
            problems.append((qc, pair, [(kc_parts[0], kc_parts[1], None)]))
        for pair, o in enumerate(attend(problems)):
            oc_ref[0, :, pair * LANES:(pair + 1) * LANES] = o.astype(oc_ref.dtype)


def _attention_core(p_lat, p_ctx, sink, cos, sin, with_ctx_out):
    b, l, n = p_lat.shape
    lc = p_ctx.shape[1]
    nkv = ATT_KV_HEADS
    wblk = n // nkv
    nq = wblk * (ATT_HEADS // nkv) // (ATT_HEADS // nkv + 2)
    const2 = lambda bi, h: (0, 0)
    out_specs = [pl.BlockSpec((1, l, nq), lambda bi, h: (bi, 0, h))]
    out_shape = [jax.ShapeDtypeStruct((b, l, nkv * nq), BF16)]
    if with_ctx_out:
        out_specs.append(pl.BlockSpec((1, lc, nq), lambda bi, h: (bi, 0, h)))
        out_shape.append(jax.ShapeDtypeStruct((b, lc, nkv * nq), BF16))
    outs = pl.pallas_call(
        functools.partial(_att_kernel, with_ctx_out=with_ctx_out),
        grid=(b, nkv),
        in_specs=[pl.BlockSpec(memory_space=pltpu.SMEM),
                  pl.BlockSpec((l, wblk), const2), pl.BlockSpec((l, wblk), const2),
                  pl.BlockSpec((1, l, wblk), lambda bi, h: (bi, 0, h)),
                  pl.BlockSpec((1, lc, wblk), lambda bi, h: (bi, 0, h))],
        out_specs=out_specs, out_shape=out_shape,
        scratch_shapes=[pltpu.VMEM((l, nq), BF16), pltpu.VMEM((2, l, LANES), BF16), pltpu.VMEM((2, l, 2 * LANES), BF16),
                        pltpu.VMEM((2, lc, LANES), BF16), pltpu.VMEM((2, lc, 2 * LANES), BF16),
                        pltpu.VMEM((3, ATT_BLOCK, ATT_BLOCK + 2 * ATT_WINDOW), F32)],
        compiler_params=_cparams(("parallel", "parallel")),
        name="attention_core",
    )(sink, cos, sin, p_lat, p_ctx)
    return (outs[0], outs[1]) if with_ctx_out else (outs[0], None)


def _pool_kernel(x_ref, mod_ref, g_ref, o_ref, rinv_scr, up_scr):
    n, d = x_ref.shape[1], x_ref.shape[2]
    ng = len(POOL_WINDOWS)
    gw = d // ng
    pad = POOL_PAD
    m = mod_ref[0]
    x = x_ref[0]
    rinv_scr[...] = lax.rsqrt(jnp.mean(x * x, axis=-1, keepdims=True) + EPS)
    up_scr[0:pad, :] = jnp.zeros((pad, gw), F32)
    up_scr[pad + n:pad + n + pad, :] = jnp.zeros((pad, gw), F32)
    t = lax.broadcasted_iota(jnp.int32, (n, 1), 0)
    for gi, w in enumerate(POOL_WINDOWS):
        cols = slice(gi * gw, (gi + 1) * gw)
        u = x_ref[0, :, cols] * rinv_scr[...] * g_ref[:, cols] * (1.0 + m[1:2, cols]) + m[0:1, cols]
        up_scr[pad:pad + n, :] = u
        lo, hi = w // 2, w - 1 - w // 2
        tot = up_scr[pad - lo:pad - lo + n, :]
        for dlt in range(-lo + 1, hi + 1):
            tot = tot + up_scr[pad + dlt:pad + dlt + n, :]
        cnt = (jnp.minimum(t + hi + 1, n) - jnp.maximum(t - lo, 0)).astype(F32)
        o_ref[0, :, cols] = (tot / cnt - u).astype(o_ref.dtype)


def _pool_core(x, mod, g):
    bx, n, d = x.shape
    gw = d // len(POOL_WINDOWS)
    return pl.pallas_call(
        _pool_kernel,
        grid=(bx,),
        in_specs=[pl.BlockSpec((1, n, d), lambda b: (b, 0, 0)),
                  pl.BlockSpec((1, N_MOD, d), lambda b: (b, 0, 0)),
                  pl.BlockSpec((1, d), lambda b: (0, 0))],
        out_specs=pl.BlockSpec((1, n, d), lambda b: (b, 0, 0)),
        out_shape=jax.ShapeDtypeStruct((bx, n, d), BF16),
        scratch_shapes=[pltpu.VMEM((n, 1), F32), pltpu.VMEM((n + 2 * POOL_PAD, gw), F32)],
        compiler_params=_cparams(("parallel",)),
        name="pool_core",
    )(x, mod, g.reshape(1, d))


def _unit_tri_inverses(mats):
    c = mats[0].shape[0]
    ii = lax.broadcasted_iota(jnp.int32, (c, c), 0)
    jj = lax.broadcasted_iota(jnp.int32, (c, c), 1)
    blk = lambda t, k: lax.shift_right_logical(t, k)
    eye = (ii == jj).astype(F32)
    pair = (blk(ii, 1) == blk(jj, 1)) & (ii != jj)
    ds = [eye - jnp.where(pair, a, 0.0) for a in mats]
    k = 1
    while (2 << k) <= c:
        m = (blk(ii, k + 1) == blk(jj, k + 1)) & (blk(ii, k) != blk(jj, k))
        xs = [_dot_t(jnp.where(m, a, 0.0), d) for a, d in zip(mats, ds)]
        ys = [_dot_t(d, x) for d, x in zip(ds, xs)]
        ds = [d - y for d, y in zip(ds, ys)]
        k += 1
    return ds


def _dot_t(a, b):
    return _dot(a.astype(BF16), b.astype(BF16))


def _cumsum_rows(tri, x):
    hi = x.astype(BF16)
    lo = (x - hi.astype(F32)).astype(BF16)
    r = _dot(tri, jnp.concatenate([hi, lo], axis=1))
    return r[:, 0:LANES] + r[:, LANES:2 * LANES]


DN_HEADS_PER_STEP = 2


def _dn_kernel(alog_ref, dtb_ref, ng_ref, cw_ref_q, cw_ref_k, cw_ref_v,
                ql_ref, kl_ref, vl_ref, gfl_ref, gbl_ref, gl_ref,
                qc_ref, kc_ref, vc_ref, gfc_ref, gbc_ref, gc_ref,
                yl_ref, *rest, with_ctx_out):
    if with_ctx_out:
        yc_ref, rest = rest[0], rest[1:]
    else:
        yc_ref = None
    q_scr, k_scr, v_scr, g_scr, xp_scr, sf_scr, sb_scr, yacc_scr = rest[:8]
    fwd_bufs, bwd_bufs = rest[8:12], rest[12:16]
    c = DN_CHUNK
    nh = DN_HEADS
    hp = DN_HEADS_PER_STEP
    l = ql_ref.shape[1]
    lc = qc_ref.shape[1]
    dk = ql_ref.shape[2] // hp
    hg = pl.program_id(1)
    kw = DN_CONV_W
    cpad = 8
    n_c, n_l = lc // c, l // c
    nck = n_c + n_l
    heads =[(p, slice(p * dk, (p + 1) * dk)) for p in range(hp)]

    def conv_seq(src_ref, cw_ref, dst_scr, base, n, norm, scale):
        xp_scr[0:cpad, :] = jnp.zeros((cpad, hp * dk), F32)
        xp_scr[cpad:cpad + n, :] = src_ref[0].astype(F32)
        xp_scr[cpad + n:cpad + n + cpad, :] = jnp.zeros((cpad, hp * dk), F32)
        rb = _pick_tile(n, 128)
        for start in range(0, n, rb):
            y = None
            for t in range(kw):
                o = start + cpad - kw // 2 + t
                term = xp_scr[o:o + rb, :] * cw_ref[t:t + 1, :]
                y = term if y is None else y + term
            y = _silu(y)
            if norm:
                y = jnp.concatenate([y[:, hs] * (lax.rsqrt(jnp.sum(y[:, hs] * y[:, hs], axis=-1, keepdims=True) + EPS)
                                                 * scale) for _, hs in heads], axis=1)
            dst_scr[base + start:base + start + rb, :] = y.astype(BF16)

    for src_c, src_l, cw, dst, norm, scale in ((qc_ref, ql_ref, cw_ref_q, q_scr, True, dk ** -0.5),
                                               (kc_ref, kl_ref, cw_ref_k, k_scr, True, 1.0),
                                               (vc_ref, vl_ref, cw_ref_v, v_scr, False, 1.0)):
        conv_seq(src_c, cw, dst, 0, lc, norm, scale)
        conv_seq(src_l, cw, dst, lc, l, norm, scale)
    g_scr[0:lc, :] = gc_ref[0]
    g_scr[lc:lc + l, :] = gl_ref[0]

    lane = lax.broadcasted_iota(jnp.int32, (1, LANES), 1)
    neg_a = -jnp.exp(alog_ref[...])
    dtb = dtb_ref[...]
    ii = lax.broadcasted_iota(jnp.int32, (c, c), 0)
    jj = lax.broadcasted_iota(jnp.int32, (c, c), 1)
    tri_l = (ii >= jj).astype(BF16)
    tri_u = (ii <= jj).astype(BF16)

    def lane_col(x, idx):
        return jnp.broadcast_to(jnp.sum(jnp.where(lane == idx, x, 0.0), axis=1, keepdims=True), (c, LANES))

    def prep_group(cis):
        chains = []
        for ci in cis:
            srows = pl.ds(pl.multiple_of(ci * c, c), c)
            graw = g_scr[srows, :]
            sig = jax.nn.sigmoid(graw)
            la_all = neg_a * jax.nn.softplus(graw + dtb)
            for p, hs in heads:
                h = hg * hp + p
                q, k = q_scr[srows, hs], k_scr[srows, hs]
                qf, kf, vf = q.astype(F32), k.astype(F32), v_scr[srows, hs].astype(F32)
                kk, qk = _dot_nt(k, k), _dot_nt(q, k)
                for upper, bufs in ((False, fwd_bufs), (True, bwd_bufs)):
                    beta = lane_col(sig, h + (nh if upper else 0))
                    la = lane_col(la_all, h + (3 * nh if upper else 2 * nh))
                    chains.append(dict(upper=upper, bufs=bufs, ci=ci, p=p, hs=hs, srows=srows, qf=qf, kf=kf, vf=vf,
                                       kk=kk, qk=qk, beta=beta, la=la))
        for ch in chains:
            ch["g"] = _cumsum_rows(tri_u if ch["upper"] else tri_l, ch["la"])
        for ch in chains:
            g = ch["g"]
            keep = (ii <= jj) if ch["upper"] else (ii >= jj)
            g_row = g.T[0:c, 0:c]
            ch["decay"] = jnp.where(keep, jnp.exp(jnp.where(keep, g[:, 0:c] - g_row, 0.0)), 0.0)
        t_invs = _unit_tri_inverses([jnp.where(ii != jj, ch["kk"] * ch["beta"][:, 0:c] * ch["decay"], 0.0)
                                     for ch in chains])
        sols = []
        for ch, t_inv in zip(chains, t_invs):
            ch["eg"] = jnp.exp(ch["g"])
            rhs = jnp.concatenate([ch["vf"] * ch["beta"], ch["kf"] * (ch["beta"] * ch["eg"])], axis=1).astype(BF16)
            sols.append(_dot(t_inv.astype(BF16), rhs))
        for ch, sol in zip(chains, sols):
            g = ch["g"]
            ch["g_last"] = g[0:1, :] if ch["upper"] else g[c - 1:c, :]
            ch["uw"] = sol.astype(BF16)
            ch["kgt"] = (ch["kf"] * jnp.exp(ch["g_last"] - g)).T.astype(BF16)
            ch["attn"] = (ch["qk"] * ch["decay"]).astype(BF16)
        kus = [_dot(ch["kgt"], ch["uw"]) for ch in chains]
        aus = [_dot(ch["attn"], ch["uw"]) for ch in chains]
        for ch, ku, au in zip(chains, kus, aus):
            mq_scr, n_scr, o_scr, gl_scr = ch["bufs"]
            ci, p, hs = ch["ci"], ch["p"], ch["hs"]
            mq_scr[pl.ds(pl.multiple_of(ci * (dk + c), c), dk), hs] = ku[:, dk:2 * dk].astype(BF16)
            mq_scr[pl.ds(pl.multiple_of(ci * (dk + c) + dk, c), c), hs] = (ch["qf"] * ch["eg"] - au[:, dk:2 * dk]).astype(BF16)
            n_scr[pl.ds(pl.multiple_of(ci * dk, dk), dk), hs] = ku[:, 0:dk].astype(BF16)
            o_scr[ch["srows"], hs] = au[:, 0:dk]
            gl_scr[p, pl.ds(pl.multiple_of(ci * 8, 8), 8), :] = jnp.broadcast_to(jnp.exp(ch["g_last"]), (8, LANES))

    nb = next(t for t in (6, 4, 2, 1) if nck % t == 0)

    def prep_body(r, carry):
        prep_group([r * nb + t for t in range(nb)])
        return carry

    lax.fori_loop(0, nck // nb, prep_body, 0)

    sf_scr[...] = jnp.zeros_like(sf_scr)
    sb_scr[...] = jnp.zeros_like(sb_scr)
    norm_g = ng_ref[...]

    def step_all(base, chunk_ids, gate_refs):
        chains = []
        for ci, gate_ref, s_scr, bufs in zip(chunk_ids, gate_refs, (sf_scr, sb_scr), (fwd_bufs, bwd_bufs)):
            gci = base // c + ci
            for p, hs in heads:
                chains.append(dict(rows=pl.ds(pl.multiple_of(ci * c, c), c), gci=gci, gate_ref=gate_ref, s_scr=s_scr,
                                   bufs=bufs, p=p, hs=hs, srows=pl.ds(pl.multiple_of(gci * c, c), c)))
        for ch in chains:
            mq_scr, n_scr, o_scr, gl_scr = ch["bufs"]
            ch["s"] = ch["s_scr"][ch["p"]]
            mq = mq_scr[pl.ds(pl.multiple_of(ch["gci"] * (dk + c), c), dk + c), ch["hs"]]
            ch["r"] = _dot(mq, ch["s"].astype(BF16))
        for ch in chains:
            mq_scr, n_scr, o_scr, gl_scr = ch["bufs"]
            gl = gl_scr[ch["p"], pl.ds(pl.multiple_of(ch["gci"] * 8, 8), 8), :]
            incr = n_scr[pl.ds(pl.multiple_of(ch["gci"] * dk, dk), dk), ch["hs"]].astype(F32)
            ch["s_scr"][ch["p"]] = ch["s"] * gl[0:1, :] + (incr - ch["r"][0:dk, :])
        outs = []
        for ch in chains:
            if ch["gate_ref"] is None:
                outs.append(None)
                continue
            o = ch["r"][dk:dk + c, :] + ch["bufs"][2][ch["srows"], ch["hs"]]
            y = o * lax.rsqrt(jnp.mean(o * o, axis=-1, keepdims=True) + EPS) * norm_g
            outs.append(y * _silu(ch["gate_ref"][0, ch["rows"], ch["hs"]].astype(F32)))
        return chains, outs

    def make_step(base, gf_ref, gb_ref, y_ref):
        def step(cf, cb, second_half):
            chains, outs = step_all(base, (cf, cb), (gf_ref, gb_ref))
            if second_half is None:
                return
            for ch, y in zip(chains, outs):
                if second_half:
                    y_ref[0, ch["rows"], ch["hs"]] = (yacc_scr[ch["rows"], ch["hs"]] + y).astype(y_ref.dtype)
                else:
                    yacc_scr[ch["rows"], ch["hs"]] = y
        return step

    if with_ctx_out:
        _run_bidirectional(n_c, make_step(0, gfc_ref, gbc_ref, yc_ref), True)
    else:
        _run_bidirectional(n_c, make_step(0, None, None, None), False)
    _run_bidirectional(n_l, make_step(lc, gfl_ref, gbl_ref, yl_ref), True)


def _deltanet_core(p_lat, g_lat, p_ctx, g_ctx, conv_w, a_lanes, dtb_lanes, norm_g, with_ctx_out):
    b, l, _ = p_lat.shape
    lc = p_ctx.shape[1]
    nh = DN_HEADS
    hp = DN_HEADS_PER_STEP
    ng = nh // hp
    dk = norm_g.shape[-1]
    w = hp * dk

    def specs(n):
        sp = [pl.BlockSpec((1, n, w), lambda bi, h, k=k: (bi, 0, k * ng + h)) for k in range(5)]
        return sp + [pl.BlockSpec((1, n, LANES), lambda bi, h: (bi, 0, 0))]

    const2 = lambda bi, h: (0, 0)
    kw = conv_w.shape[0]
    c = DN_CHUNK
    ltot = lc + l
    nck = ltot // c
    dir_bufs = [pltpu.VMEM((nck * (dk + c), w), BF16), pltpu.VMEM((nck * dk, w), BF16), pltpu.VMEM((ltot, w), F32),
                pltpu.VMEM((hp, nck * 8, LANES), F32)]
    out_specs = [pl.BlockSpec((1, l, w), lambda bi, h: (bi, 0, h))]
    out_shape = [jax.ShapeDtypeStruct((b, l, nh * dk), BF16)]
    if with_ctx_out:
        out_specs.append(pl.BlockSpec((1, lc, w), lambda bi, h: (bi, 0, h)))
        out_shape.append(jax.ShapeDtypeStruct((b, lc, nh * dk), BF16))
    outs = pl.pallas_call(
        functools.partial(_dn_kernel, with_ctx_out=with_ctx_out),
        grid=(b, ng),
        in_specs=[pl.BlockSpec((1, LANES), const2), pl.BlockSpec((1, LANES), const2), pl.BlockSpec((1, dk), const2)]
                 + [pl.BlockSpec((kw, w), lambda bi, h, k=k: (0, k * ng + h)) for k in range(3)]
                 + specs(l) + specs(lc),
        out_specs=out_specs, out_shape=out_shape,
        scratch_shapes=[pltpu.VMEM((ltot, w), BF16), pltpu.VMEM((ltot, w), BF16), pltpu.VMEM((ltot, w), BF16),
                        pltpu.VMEM((ltot, LANES), F32), pltpu.VMEM((max(l, lc) + 16, w), F32),
                        pltpu.VMEM((hp, dk, dk), F32), pltpu.VMEM((hp, dk, dk), F32),
                        pltpu.VMEM((max(l, lc), w), F32)] + 2 * dir_bufs,
        compiler_params=pltpu.CompilerParams(dimension_semantics=("parallel", "parallel"),
                                             vmem_limit_bytes=DN_VMEM_LIMIT),
        name="deltanet_core",
    )(a_lanes, dtb_lanes, norm_g.reshape(1, dk), conv_w, conv_w, conv_w,
      *([p_lat] * 5), g_lat, *([p_ctx] * 5), g_ctx)
    return (outs[0], outs[1]) if with_ctx_out else (outs[0], None)


def kernel(x, c, ctx, c_ctx, ada_w, ada_b, mix_pre_g, mix_post_g, mlp_pre_g, mlp_post_g, mlp_w1, mlp_w2, ret_w_in, ret_decay_logit, ret_w_out, att_w_in, att_sink, att_w_out, pool_w, pool_b, pool_scale, dn_w_in, dn_conv_w, dn_a_log, dn_dt_bias, dn_norm_g, dn_w_out):
    b, l, d = x.shape
    lc = ctx.shape[1]
    depth = ada_w.shape[0]
    n_mixers = 4

    rows = -(-(b + 1) // 8) * 8
    cond = jnp.zeros((rows, d), F32).at[:b].set(c).at[b].set(c_ctx)
    mods = _modulation(cond, ada_w, ada_b)
    xc = ctx.reshape(1, b * lc, d)

    for i in range(depth):
        kind, inst = i % n_mixers, i // n_mixers
        need_ctx = i < depth - 1
        m_lat = mods[i, :b].reshape(b, N_MOD, d)
        m_ctx = mods[i, b:b + 1].reshape(1, N_MOD, d)
        w1, w2 = mlp_w1[i].astype(BF16), mlp_w2[i].astype(BF16)
        post_kw = {}
        if kind == 0:
            w_in = ret_w_in[inst].astype(BF16)
            p_lat = _inproj(x, m_lat, mix_pre_g[i], w_in, name="ret_inproj")
            p_ctx = _inproj(xc, m_ctx, mix_pre_g[i], w_in, name="ret_inproj_ctx").reshape(b, lc, -1)
            dk = d // RET_HEADS
            cos, sin = _rope_tables(l, dk, 1)
            a_lat, a_ctx = _retention_core(p_lat, p_ctx, ret_decay_logit[inst], cos, sin)
            w_out = ret_w_out[inst].astype(BF16)
        elif kind == 1:
            grp = ATT_HEADS // ATT_KV_HEADS
            dh = d // ATT_HEADS
            nq, nkv = ATT_HEADS * dh, ATT_KV_HEADS * dh
            wq = att_w_in[inst][:, :nq].reshape(d, ATT_KV_HEADS, grp * dh)
            wk = att_w_in[inst][:, nq:nq + nkv].reshape(d, ATT_KV_HEADS, dh)
            wv = att_w_in[inst][:, nq + nkv:].reshape(d, ATT_KV_HEADS, dh)
            w_in = jnp.concatenate([wq, wk, wv], axis=2).reshape(d, nq + 2 * nkv).astype(BF16)
            p_lat = _inproj(x, m_lat, mix_pre_g[i], w_in, name="att_inproj")
            p_ctx = _inproj(xc, m_ctx, mix_pre_g[i], w_in, name="att_inproj_ctx").reshape(b, lc, -1)
            cos_h, sin_h = _rope_tables(l, dh, 1)
            one, zero = jnp.ones((l, dh), F32), jnp.zeros((l, dh), F32)
            qs = dh ** -0.5
            cos = jnp.concatenate([jnp.tile(cos_h, (1, grp)) * qs, cos_h, one], axis=1)
            sin = jnp.concatenate([jnp.tile(sin_h, (1, grp)) * qs, sin_h, zero], axis=1)
            a_lat, a_ctx = _attention_core(p_lat, p_ctx, att_sink[inst], cos, sin, need_ctx)
            w_out = att_w_out[inst].astype(BF16)
        elif kind == 2:
            a_lat = _pool_core(x, m_lat, mix_pre_g[i])
            a_ctx = _pool_core(xc.reshape(b, lc, d), jnp.broadcast_to(m_ctx, (b, N_MOD, d)), mix_pre_g[i]) if need_ctx else None
            w_out = pool_w[inst].astype(BF16)
            post_kw = dict(pool_bias=pool_b[inst], pool_scale=pool_scale[inst])
        else:
            nh = DN_HEADS
            dk = d // nh
            nqkv = 3 * nh * dk
            wd = dn_w_in[inst]
            w_main = jnp.concatenate([wd[:, :nqkv], wd[:, nqkv + 4 * nh:]], axis=1).astype(BF16)
            w_gate = jnp.zeros((d, LANES), F32).at[:, :4 * nh].set(wd[:, nqkv:nqkv + 4 * nh]).astype(BF16)
            p_lat, g_lat = _inproj(x, m_lat, mix_pre_g[i], w_main, aux_w=w_gate, name="dn_inproj")
            p_ctx, g_ctx = _inproj(xc, m_ctx, mix_pre_g[i], w_main, aux_w=w_gate, name="dn_inproj_ctx")
            p_ctx, g_ctx = p_ctx.reshape(b, lc, -1), g_ctx.reshape(b, lc, -1)
            a_lanes = jnp.zeros((1, LANES), F32).at[0, 2 * nh:4 * nh].set(dn_a_log[inst].reshape(-1))
            dtb_lanes = jnp.zeros((1, LANES), F32).at[0, 2 * nh:4 * nh].set(dn_dt_bias[inst].reshape(-1))
            a_lat, a_ctx = _deltanet_core(p_lat, g_lat, p_ctx, g_ctx, dn_conv_w[inst], a_lanes, dtb_lanes,
                                          dn_norm_g[inst], need_ctx)
            w_out = dn_w_out[inst].astype(BF16)

        x = _post(a_lat, x, m_lat, mix_post_g[i], mlp_pre_g[i], mlp_post_g[i], w_out, w1, w2,
                  name="post", **post_kw)
        if need_ctx:
            xc = _post(a_ctx.reshape(1, b * lc, -1), xc, m_ctx, mix_post_g[i], mlp_pre_g[i], mlp_post_g[i],
                       w_out, w1, w2, name="post_ctx", **post_kw)
    return x
```

```python
import functools

import jax
import jax.numpy as jnp
from jax import lax
from jax.experimental import pallas as pl
from jax.experimental.pallas import tpu as pltpu

F32 = jnp.float32
BF16 = jnp.bfloat16
HI = lax.Precision.HIGHEST

EPS = 1e-6
NEG_INF = -1e30
ROPE_BASE = 10000.0
GRID_W = 64
N_MOD = 6
RET_HEADS = 8
RET_CHUNK = 128
ATT_HEADS = 16
ATT_KV_HEADS = 4
ATT_WINDOW = 128
ATT_BLOCK = 128
POOL_WINDOWS = (2, 4, 8, 16)
POOL_PAD = 16
DN_HEADS = 8
DN_CHUNK = 128
DN_CONV_W = 5

LANES = 128
VMEM_LIMIT = 48 * 1024 * 1024
DN_VMEM_LIMIT = 56 * 1024 * 1024
INPROJ_MAX_TN = 2560


def _cparams(sem):
    return pltpu.CompilerParams(dimension_semantics=sem, vmem_limit_bytes=VMEM_LIMIT)


def _dot(a, b):
    return jnp.dot(a, b, preferred_element_type=F32)


def _dot_nt(a, b):
    return lax.dot_general(a, b, (((1,), (1,)), ((), ())), preferred_element_type=F32)


def _dot_hi(a, b):
    return jnp.dot(a, b, preferred_element_type=F32, precision=HI)


def _rms(x):
    return x * lax.rsqrt(jnp.mean(x * x, axis=-1, keepdims=True) + EPS)


def _silu(x):
    return x * jax.nn.sigmoid(x)


def _pick_tile(n, pref):
    t = min(pref, n)
    while n % t:
        t //= 2
    return t


def _mod_kernel(c_ref, w_ref, b_ref, o_ref):
    o_ref[0] = _dot_hi(_silu(c_ref[...]), w_ref[0]) + b_ref[0]


def _modulation(cond, ada_w, ada_b):
    depth, d, n = ada_w.shape
    rows = cond.shape[0]
    tn = _pick_tile(n, 1536)
    return pl.pallas_call(
        _mod_kernel,
        grid=(depth, n // tn),
        in_specs=[pl.BlockSpec((rows, d), lambda i, j: (0, 0)),
                  pl.BlockSpec((1, d, tn), lambda i, j: (i, 0, j)),
                  pl.BlockSpec((1, 1, tn), lambda i, j: (i, 0, j))],
        out_specs=pl.BlockSpec((1, rows, tn), lambda i, j: (i, 0, j)),
        out_shape=jax.ShapeDtypeStruct((depth, rows, n), F32),
        compiler_params=_cparams(("parallel", "parallel")),
        name="modulation",
    )(cond, ada_w, ada_b.reshape(depth, 1, n))


def _inproj_kernel(x_ref, mod_ref, g_ref, w_ref, *rest, has_aux):
    if has_aux:
        wa_ref, o_ref, oa_ref, u_scr = rest
    else:
        o_ref, u_scr = rest

    @pl.when(pl.program_id(2) == 0)
    def _():
        m = mod_ref[0]
        u = _rms(x_ref[0]) * g_ref[...] * (1.0 + m[1:2]) + m[0:1]
        u_scr[...] = u.astype(BF16)
        if has_aux:
            oa_ref[0] = _dot(u_scr[...], wa_ref[...])

    o_ref[0] = _dot(u_scr[...], w_ref[...]).astype(o_ref.dtype)


def _inproj(x, mod, g, w, aux_w=None, name="inproj"):
    bx, lx, d = x.shape
    n = w.shape[1]
    tm = _pick_tile(lx, 1024)
    tn = max(t for t in range(LANES, min(n, INPROJ_MAX_TN) + 1, LANES) if n % t == 0)
    has_aux = aux_w is not None
    in_specs = [pl.BlockSpec((1, tm, d), lambda b, i, j: (b, i, 0)),
                pl.BlockSpec((1, N_MOD, d), lambda b, i, j: (b, 0, 0)),
                pl.BlockSpec((1, d), lambda b, i, j: (0, 0)),
                pl.BlockSpec((d, tn), lambda b, i, j: (0, j))]
    out_specs = [pl.BlockSpec((1, tm, tn), lambda b, i, j: (b, i, j))]
    out_shape = [jax.ShapeDtypeStruct((bx, lx, n), BF16)]
    args = [x, mod, g.reshape(1, d), w]
    if has_aux:
        na = aux_w.shape[1]
        in_specs.append(pl.BlockSpec((d, na), lambda b, i, j: (0, 0)))
        out_specs.append(pl.BlockSpec((1, tm, na), lambda b, i, j: (b, i, 0)))
        out_shape.append(jax.ShapeDtypeStruct((bx, lx, na), F32))
        args.append(aux_w)
    outs = pl.pallas_call(
        functools.partial(_inproj_kernel, has_aux=has_aux),
        grid=(bx, lx // tm, n // tn),
        in_specs=in_specs, out_specs=out_specs, out_shape=out_shape,
        scratch_shapes=[pltpu.VMEM((tm, d), BF16)],
        compiler_params=_cparams(("parallel", "parallel", "arbitrary")),
        name=name,
    )(*args)
    return outs if has_aux else outs[0]


def _post_kernel(a_ref, x_ref, mod_ref, gpost_ref, gpre2_ref, gpost2_ref, wo_ref, *rest, grouped, tf):
    if grouped:
        pb_ref, ps_ref, w1_ref, w2_ref, o_ref = rest
    else:
        w1_ref, w2_ref, o_ref = rest
    a = a_ref[0]
    if grouped:
        ng, gw, _ = wo_ref.shape
        y = jnp.concatenate([_dot(a[:, g * gw:(g + 1) * gw], wo_ref[g]) for g in range(ng)], axis=1)
        y = (y + pb_ref[...]) * ps_ref[...]
    else:
        y = _dot(a, wo_ref[...])
    m = mod_ref[0]
    x1 = x_ref[0] + m[2:3] * (_rms(y) * gpost_ref[...])
    u = (_rms(x1) * gpre2_ref[...] * (1.0 + m[4:5]) + m[3:4]).astype(BF16)
    acc = None
    for j in range(w1_ref.shape[1] // tf):
        h = jnp.square(jnp.maximum(_dot(u, w1_ref[:, j * tf:(j + 1) * tf]), 0.0)).astype(BF16)
        t = _dot(h, w2_ref[j * tf:(j + 1) * tf, :])
        acc = t if acc is None else acc + t
    o_ref[0] = x1 + m[5:6] * (_rms(acc) * gpost2_ref[...])


def _post(a, x, mod, gpost, gpre2, gpost2, w_out, w1, w2, pool_bias=None, pool_scale=None, name="post"):
    bx, lx, d = x.shape
    ka = a.shape[2]
    dff = w1.shape[1]
    tm = _pick_tile(lx, 512)
    tf = _pick_tile(dff, 512)
    grouped = pool_bias is not None
    row = lambda v: v.reshape(1, d)
    const2 = lambda b, i: (0, 0)
    once = pl.Buffered(1)
    in_specs = [pl.BlockSpec((1, tm, ka), lambda b, i: (b, i, 0)),
                pl.BlockSpec((1, tm, d), lambda b, i: (b, i, 0)),
                pl.BlockSpec((1, N_MOD, d), lambda b, i: (b, 0, 0)),
                pl.BlockSpec((1, d), const2), pl.BlockSpec((1, d), const2), pl.BlockSpec((1, d), const2)]
    args = [a, x, mod, row(gpost), row(gpre2), row(gpost2), w_out]
    if grouped:
        in_specs += [pl.BlockSpec(w_out.shape, lambda b, i: (0, 0, 0), pipeline_mode=once),
                     pl.BlockSpec((1, d), const2), pl.BlockSpec((1, d), const2)]
        args += [row(pool_bias), row(pool_scale)]
    else:
        in_specs += [pl.BlockSpec(w_out.shape, const2, pipeline_mode=once)]
    in_specs += [pl.BlockSpec((d, dff), const2, pipeline_mode=once), pl.BlockSpec((dff, d), const2, pipeline_mode=once)]
    args += [w1, w2]
    return pl.pallas_call(
        functools.partial(_post_kernel, grouped=grouped, tf=tf),
        grid=(bx, lx // tm),
        in_specs=in_specs,
        out_specs=pl.BlockSpec((1, tm, d), lambda b, i: (b, i, 0)),
        out_shape=jax.ShapeDtypeStruct((bx, lx, d), F32),
        compiler_params=_cparams(("parallel", "parallel")),
        name=name,
    )(*args)


def _rope_tables(n, head_dim, n_rep, scale=1.0):
    quarter = head_dim // 4
    inv = ROPE_BASE ** (-jnp.arange(quarter, dtype=F32) / quarter)
    t = jnp.arange(n, dtype=jnp.int32)
    ang_r = (t // GRID_W).astype(F32)[:, None] * inv[None, :]
    ang_c = (t % GRID_W).astype(F32)[:, None] * inv[None, :]
    cos = jnp.concatenate([jnp.cos(ang_r), jnp.cos(ang_r), jnp.cos(ang_c), jnp.cos(ang_c)], axis=1)
    sin = jnp.concatenate([-jnp.sin(ang_r), jnp.sin(ang_r), -jnp.sin(ang_c), jnp.sin(ang_c)], axis=1)
    return jnp.tile(cos, (1, n_rep)) * scale, jnp.tile(sin, (1, n_rep)) * scale


def _rope(x, cos, sin, quarter):
    r = lax.broadcasted_iota(jnp.int32, (LANES, LANES), 0)
    col = lax.broadcasted_iota(jnp.int32, (LANES, LANES), 1)
    partner = jnp.where((col & (2 * quarter - 1)) < quarter, col + quarter, col - quarter)
    perm = (r == partner).astype(BF16)
    swapped = jnp.concatenate([_dot(x[:, g * LANES:(g + 1) * LANES], perm) for g in range(x.shape[1] // LANES)], axis=1)
    return x.astype(F32) * cos + swapped * sin


def _head_scalar(vec, h):
    lane = lax.broadcasted_iota(jnp.int32, vec.shape, 1)
    return jnp.sum(jnp.where(lane == h, vec, 0.0), axis=1, keepdims=True)


def _log_sigmoid(x):
    return jnp.minimum(x, 0.0) - jnp.log1p(jnp.exp(-jnp.abs(x)))


def _run_bidirectional(n, step, want_out):
    unroll = 4 if n % 8 == 0 else 2 if n % 4 == 0 else 1
    if want_out:
        assert n % 2 == 0
        lax.fori_loop(0, n // 2, lambda i, c: step(i, n - 1 - i, False) or c, 0, unroll=unroll)
        lax.fori_loop(n // 2, n, lambda i, c: step(i, n - 1 - i, True) or c, 0, unroll=unroll)
    else:
        lax.fori_loop(0, n, lambda i, c: step(i, n - 1 - i, None) or c, 0, unroll=unroll)


def _ret_kernel(dl_ref, cos_ref, sin_ref, ql_ref, kl_ref, vl_ref, gfl_ref, gbl_ref,
                qc_ref, kc_ref, vc_ref, gfc_ref, gbc_ref, yl_ref, yc_ref,
                q_scr, k_scr, v_scr, p_scr, s_all_scr):
    c = RET_CHUNK
    l = ql_ref.shape[1]
    lc = qc_ref.shape[1]
    dk = ql_ref.shape[2]
    h = pl.program_id(1)
    scale = dk ** -0.5

    ls = _log_sigmoid(dl_ref[...])
    lgf = _head_scalar(ls[0:1], h)
    lgb = _head_scalar(ls[1:2], h)

    q_scr[0:lc, :] = (qc_ref[0].astype(F32) * scale).astype(BF16)
    k_scr[0:lc, :] = kc_ref[0]
    rb = _pick_tile(l, 256)

    def rope_body(r, carry):
        rows = pl.ds(pl.multiple_of(r * rb, rb), rb)
        dst = pl.ds(pl.multiple_of(lc + r * rb, rb), rb)
        cs, sn = cos_ref[rows, :], sin_ref[rows, :]
        q_scr[dst, :] = (_rope(ql_ref[0, rows, :], cs, sn, dk // 4) * scale).astype(BF16)
        k_scr[dst, :] = _rope(kl_ref[0, rows, :], cs, sn, dk // 4).astype(BF16)
        return carry

    lax.fori_loop(0, l // rb, rope_body, 0, unroll=2 - (l // rb) % 2)

    ii = lax.broadcasted_iota(jnp.int32, (c, c), 0).astype(F32)
    jj = lax.broadcasted_iota(jnp.int32, (c, c), 1).astype(F32)
    diff = ii - jj
    dec_f = jnp.where(diff >= 0, jnp.exp(lgf * jnp.maximum(diff, 0.0)), 0.0)
    dec_b = jnp.where(diff <= 0, jnp.exp(lgb * jnp.maximum(-diff, 0.0)), 0.0)
    idx = lax.broadcasted_iota(jnp.int32, (c, 1), 0).astype(F32)
    dv = vl_ref.shape[2]
    qdec_f, cd_f = jnp.exp(lgf * (idx + 1.0)), jnp.exp(lgf * c)
    qdec_b, cd_b = jnp.exp(lgb * (c - idx)), jnp.exp(lgb * c)

    n_c, n_l = lc // c, l // c
    n = n_c + n_l
    v_scr[0:lc, :] = vc_ref[0]
    v_scr[lc:lc + l, :] = vl_ref[0]

    lane_idx = lax.broadcasted_iota(jnp.int32, (1, c), 1).astype(F32)
    kdec_f_row, kdec_b_row = jnp.exp(lgf * (c - 1.0 - lane_idx)), jnp.exp(lgb * lane_idx)

    grp = _pick_tile(n_l, 8)

    def incr_group(cis):
        srows = [pl.ds(pl.multiple_of(ci * c, c), c) for ci in cis]
        kts = [k_scr[sr, :].astype(F32).T for sr in srows]
        lhs = [jnp.concatenate([kt * kdec_f_row, kt * kdec_b_row], axis=0).astype(BF16) for kt in kts]
        ps = [_dot(a, v_scr[sr, :]) for a, sr in zip(lhs, srows)]
        for ci, p in zip(cis, ps):
            p_scr[pl.ds(pl.multiple_of(ci * 2 * dk, 2 * dk), 2 * dk), :] = p

    def incr_body(r, carry):
        incr_group([n_c + r * grp + t for t in range(grp)])
        return carry

    incr_group(list(range(n_c)))
    lax.fori_loop(0, n_l // grp, incr_body, 0)

    def scan(first, count, step, s, half, cd):
        def body(t, s):
            ci = first + t * step
            s_all_scr[pl.ds(pl.multiple_of(ci * dk, dk), dk), half * dv:(half + 1) * dv] = s.astype(BF16)
            return s * cd + p_scr[pl.ds(pl.multiple_of((ci * 2 + half) * dk, dk), dk), :]
        return lax.fori_loop(0, count, body, s)

    zero = jnp.zeros((dk, dv), F32)
    scan(0, n, 1, zero, 0, cd_f)
    scan(n - 1, n_l, -1, scan(n_c - 1, n_c, -1, zero, 1, cd_b), 1, cd_b)

    def out_group(base_chunk, cis, gf_ref, gb_ref, y_ref):
        srows = [pl.ds(pl.multiple_of((base_chunk + ci) * c, c), c) for ci in cis]
        qs = [q_scr[sr, :] for sr in srows]
        ss = [_dot_nt(q, k_scr[sr, :]) for q, sr in zip(qs, srows)]
        cross = [_dot(q, s_all_scr[pl.ds(pl.multiple_of((base_chunk + ci) * dk, dk), dk), :]) for q, ci in zip(qs, cis)]
        intra = [_dot(jnp.concatenate([s * dec_f, s * dec_b], axis=0).astype(BF16), v_scr[sr, :])
                 for s, sr in zip(ss, srows)]
        for ci, it, cr in zip(cis, intra, cross):
            rows = pl.ds(pl.multiple_of(ci * c, c), c)
            y = None
            for half, (qdec, g_ref) in enumerate(((qdec_f, gf_ref), (qdec_b, gb_ref))):
                o = it[half * c:(half + 1) * c, :] + qdec * cr[:, half * dv:(half + 1) * dv]
                t = o * lax.rsqrt(jnp.mean(o * o, axis=-1, keepdims=True) + EPS) * _silu(g_ref[0, rows, :].astype(F32))
                y = t if y is None else y + t
            y_ref[0, rows, :] = y.astype(y_ref.dtype)

    out_group(0, list(range(n_c)), gfc_ref, gbc_ref, yc_ref)

    def out_body(r, carry):
        out_group(n_c, [r * grp + t for t in range(grp)], gfl_ref, gbl_ref, yl_ref)
        return carry

    lax.fori_loop(0, n_l // grp, out_body, 0)


def _retention_core(p_lat, p_ctx, decay_logit, cos, sin):
    b, l, _ = p_lat.shape
    lc = p_ctx.shape[1]
    nh = RET_HEADS
    dk = cos.shape[1]
    dv = 2 * dk

    def specs(n):
        return [pl.BlockSpec((1, n, dk), lambda bi, h: (bi, 0, h)),
                pl.BlockSpec((1, n, dk), lambda bi, h: (bi, 0, nh + h)),
                pl.BlockSpec((1, n, dv), lambda bi, h: (bi, 0, nh + h)),
                pl.BlockSpec((1, n, dv), lambda bi, h: (bi, 0, 2 * nh + h)),
                pl.BlockSpec((1, n, dv), lambda bi, h: (bi, 0, 3 * nh + h))]

    const2 = lambda bi, h: (0, 0)
    return pl.pallas_call(
        _ret_kernel,
        grid=(b, nh),
        in_specs=[pl.BlockSpec(decay_logit.shape, const2), pl.BlockSpec((l, dk), const2),
                  pl.BlockSpec((l, dk), const2)] + specs(l) + specs(lc),
        out_specs=[pl.BlockSpec((1, l, dv), lambda bi, h: (bi, 0, h)),
                   pl.BlockSpec((1, lc, dv), lambda bi, h: (bi, 0, h))],
        out_shape=[jax.ShapeDtypeStruct((b, l, nh * dv), BF16), jax.ShapeDtypeStruct((b, lc, nh * dv), BF16)],
        scratch_shapes=[pltpu.VMEM((lc + l, dk), BF16), pltpu.VMEM((lc + l, dk), BF16),
                        pltpu.VMEM((lc + l, dv), BF16),
                        pltpu.VMEM(((lc + l) // RET_CHUNK * 2 * dk, dv), F32),
                        pltpu.VMEM(((lc + l) // RET_CHUNK * dk, 2 * dv), BF16)],
        compiler_params=_cparams(("parallel", "parallel")),
        name="retention_core",
    )(decay_logit, cos, sin, *([p_lat] * 5), *([p_ctx] * 5))


def _att_kernel(sink_ref, cos_ref, sin_ref, pl_ref, pc_ref, ol_ref, *rest, with_ctx_out):
    if with_ctx_out:
        oc_ref, rest = rest[0], rest[1:]
    q_scr, k_scr, v_scr, kc_scr, vc_scr, bias_scr = rest
    grp = ATT_HEADS // ATT_KV_HEADS
    qb, win = ATT_BLOCK, ATT_WINDOW
    span = qb + 2 * win
    l = pl_ref.shape[1]
    lc = pc_ref.shape[1]
    dh = pl_ref.shape[2] // (grp + 2)
    nq = grp * dh
    nblk = l // qb
    npair = nq // LANES
    kh = pl.program_id(1)
    assert 2 * dh == LANES

    def split_kv(kv, k_dst, v_dst, rows):
        lo = lax.broadcasted_iota(jnp.int32, kv.shape, 1) < dh
        vk = pltpu.roll(kv, dh, 1)
        one = jnp.ones_like(kv)
        k_dst[0, rows, :] = jnp.where(lo, kv, 0.0).astype(BF16)
        k_dst[1, rows, :] = jnp.where(lo, 0.0, vk).astype(BF16)
        v_dst[0, rows, :] = jnp.concatenate([jnp.where(lo, vk, 0.0), jnp.where(lo, one, 0.0)], axis=1).astype(BF16)
        v_dst[1, rows, :] = jnp.concatenate([jnp.where(lo, 0.0, kv), jnp.where(lo, 0.0, one)], axis=1).astype(BF16)

    rb = _pick_tile(l, 256)

    def rope_body(r, carry):
        rows = pl.ds(pl.multiple_of(r * rb, rb), rb)
        x = _rope(pl_ref[0, rows, :], cos_ref[rows, :], sin_ref[rows, :], dh // 4)
        q_scr[rows, :] = x[:, 0:nq].astype(BF16)
        split_kv(x[:, nq:nq + 2 * dh], k_scr, v_scr, rows)
        return carry

    lax.fori_loop(0, l // rb, rope_body, 0, unroll=2 - (l // rb) % 2)
    split_kv(pc_ref[0, :, nq:nq + 2 * dh].astype(F32), kc_scr, vc_scr, slice(None))

    qi = lax.broadcasted_iota(jnp.int32, (qb, span), 0)
    kj = lax.broadcasted_iota(jnp.int32, (qb, span), 1)
    for e in range(3):
        bias_scr[e] = jnp.where(jnp.abs(kj - e * win - qi) <= win, 0.0, NEG_INF)
    lo_half = lax.broadcasted_iota(jnp.int32, (1, LANES), 1) < dh

    def attend(problems):
        scores = []
        for q, _, parts in problems:
            scores.append([[_dot_nt(q, kk[par]) if bias is None else _dot_nt(q, kk[par]) + bias
                            for kk, _, bias in parts] for par in range(2)])
        mxs = []
        for (_, pair, _), sc in zip(problems, scores):
            row = []
            for par in range(2):
                mx = jnp.full((sc[par][0].shape[0], 1), sink_ref[kh * grp + 2 * pair + par], F32)
                for s in sc[par]:
                    mx = jnp.maximum(mx, jnp.max(s, axis=-1, keepdims=True))
                row.append(mx)
            mxs.append(row)
        outs = []
        for (_, pair, parts), sc, mx in zip(problems, scores, mxs):
            acc = None
            for par in range(2):
                for s, (_, vv, _) in zip(sc[par], parts):
                    t = _dot(jnp.exp(s - mx[par]).astype(BF16), vv[par])
                    acc = t if acc is None else acc + t
            sink_e = jnp.exp(sink_ref[kh * grp + 2 * pair] - mx[0])
            sink_o = jnp.exp(sink_ref[kh * grp + 2 * pair + 1] - mx[1])
            outs.append(acc[:, 0:LANES] / (acc[:, LANES:2 * LANES] + jnp.where(lo_half, sink_e, sink_o)))
        return outs

    kc_parts = ([kc_scr[0], kc_scr[1]], [vc_scr[0], vc_scr[1]])
    nb = _pick_tile(nblk, 4)

    def block_group(r, carry):
        problems, dests = [], []
        for t in range(nb):
            bi = r * nb + t
            qs = pl.multiple_of(bi * qb, qb)
            ks = pl.multiple_of(jnp.clip(qs - win, 0, l - span), qb)
            edge = jnp.where(bi == 0, 0, jnp.where(bi == nblk - 1, 2, 1))
            bias = bias_scr[edge]
            kwin = [k_scr[par, pl.ds(ks, span), :] for par in range(2)]
            vwin = [v_scr[par, pl.ds(ks, span), :] for par in range(2)]
            for pair in range(npair):
                q = q_scr[pl.ds(qs, qb), pair * LANES:(pair + 1) * LANES]
                problems.append((q, pair, [(kwin, vwin, bias), (kc_parts[0], kc_parts[1], None)]))
                dests.append((qs, pair))
        for (qs, pair), o in zip(dests, attend(problems)):
            ol_ref[0, pl.ds(qs, qb), pair * LANES:(pair + 1) * LANES] = o.astype(ol_ref.dtype)
        return carry

    lax.fori_loop(0, nblk // nb, block_group, 0)

    if with_ctx_out:
        problems = []
        for pair in range(npair):
            qc = (pc_ref[0, :, pair * LANES:(pair + 1) * LANES].astype(F32) * dh ** -0.5).astype(BF16)
            problems.append((qc, pair, [(kc_parts[0], kc_parts[1], None)]))
        for pair, o in enumerate(attend(problems)):
            oc_ref[0, :, pair * LANES:(pair + 1) * LANES] = o.astype(oc_ref.dtype)


def _attention_core(p_lat, p_ctx, sink, cos, sin, with_ctx_out):
    b, l, n = p_lat.shape
    lc = p_ctx.shape[1]
    nkv = ATT_KV_HEADS
    wblk = n // nkv
    nq = wblk * (ATT_HEADS // nkv) // (ATT_HEADS // nkv + 2)
    const2 = lambda bi, h: (0, 0)
    out_specs = [pl.BlockSpec((1, l, nq), lambda bi, h: (bi, 0, h))]
    out_shape = [jax.ShapeDtypeStruct((b, l, nkv * nq), BF16)]
    if with_ctx_out:
        out_specs.append(pl.BlockSpec((1, lc, nq), lambda bi, h: (bi, 0, h)))
        out_shape.append(jax.ShapeDtypeStruct((b, lc, nkv * nq), BF16))
    outs = pl.pallas_call(
        functools.partial(_att_kernel, with_ctx_out=with_ctx_out),
        grid=(b, nkv),
        in_specs=[pl.BlockSpec(memory_space=pltpu.SMEM),
                  pl.BlockSpec((l, wblk), const2), pl.BlockSpec((l, wblk), const2),
                  pl.BlockSpec((1, l, wblk), lambda bi, h: (bi, 0, h)),
                  pl.BlockSpec((1, lc, wblk), lambda bi, h: (bi, 0, h))],
        out_specs=out_specs, out_shape=out_shape,
        scratch_shapes=[pltpu.VMEM((l, nq), BF16), pltpu.VMEM((2, l, LANES), BF16), pltpu.VMEM((2, l, 2 * LANES), BF16),
                        pltpu.VMEM((2, lc, LANES), BF16), pltpu.VMEM((2, lc, 2 * LANES), BF16),
                        pltpu.VMEM((3, ATT_BLOCK, ATT_BLOCK + 2 * ATT_WINDOW), F32)],
        compiler_params=_cparams(("parallel", "parallel")),
        name="attention_core",
    )(sink, cos, sin, p_lat, p_ctx)
    return (outs[0], outs[1]) if with_ctx_out else (outs[0], None)


def _pool_kernel(x_ref, mod_ref, g_ref, o_ref, rinv_scr, up_scr):
    n, d = x_ref.shape[1], x_ref.shape[2]
    ng = len(POOL_WINDOWS)
    gw = d // ng
    pad = POOL_PAD
    m = mod_ref[0]
    x = x_ref[0]
    rinv_scr[...] = lax.rsqrt(jnp.mean(x * x, axis=-1, keepdims=True) + EPS)
    up_scr[0:pad, :] = jnp.zeros((pad, gw), F32)
    up_scr[pad + n:pad + n + pad, :] = jnp.zeros((pad, gw), F32)
    t = lax.broadcasted_iota(jnp.int32, (n, 1), 0)
    for gi, w in enumerate(POOL_WINDOWS):
        cols = slice(gi * gw, (gi + 1) * gw)
        u = x_ref[0, :, cols] * rinv_scr[...] * g_ref[:, cols] * (1.0 + m[1:2, cols]) + m[0:1, cols]
        up_scr[pad:pad + n, :] = u
        lo, hi = w // 2, w - 1 - w // 2
        tot = up_scr[pad - lo:pad - lo + n, :]
        for dlt in range(-lo + 1, hi + 1):
            tot = tot + up_scr[pad + dlt:pad + dlt + n, :]
        cnt = (jnp.minimum(t + hi + 1, n) - jnp.maximum(t - lo, 0)).astype(F32)
        o_ref[0, :, cols] = (tot / cnt - u).astype(o_ref.dtype)


def _pool_core(x, mod, g):
    bx, n, d = x.shape
    gw = d // len(POOL_WINDOWS)
    return pl.pallas_call(
        _pool_kernel,
        grid=(bx,),
        in_specs=[pl.BlockSpec((1, n, d), lambda b: (b, 0, 0)),
                  pl.BlockSpec((1, N_MOD, d), lambda b: (b, 0, 0)),
                  pl.BlockSpec((1, d), lambda b: (0, 0))],
        out_specs=pl.BlockSpec((1, n, d), lambda b: (b, 0, 0)),
        out_shape=jax.ShapeDtypeStruct((bx, n, d), BF16),
        scratch_shapes=[pltpu.VMEM((n, 1), F32), pltpu.VMEM((n + 2 * POOL_PAD, gw), F32)],
        compiler_params=_cparams(("parallel",)),
        name="pool_core",
    )(x, mod, g.reshape(1, d))


def _unit_tri_inverses(mats):
    c = mats[0].shape[0]
    ii = lax.broadcasted_iota(jnp.int32, (c, c), 0)
    jj = lax.broadcasted_iota(jnp.int32, (c, c), 1)
    blk = lambda t, k: lax.shift_right_logical(t, k)
    eye = (ii == jj).astype(F32)
    pair = (blk(ii, 1) == blk(jj, 1)) & (ii != jj)
    ds = [eye - jnp.where(pair, a, 0.0) for a in mats]
    k = 1
    while (2 << k) <= c:
        m = (blk(ii, k + 1) == blk(jj, k + 1)) & (blk(ii, k) != blk(jj, k))
        xs = [_dot_t(jnp.where(m, a, 0.0), d) for a, d in zip(mats, ds)]
        ys = [_dot_t(d, x) for d, x in zip(ds, xs)]
        ds = [d - y for d, y in zip(ds, ys)]
        k += 1
    return ds


def _dot_t(a, b):
    return _dot(a.astype(BF16), b.astype(BF16))


def _cumsum_rows(tri, x):
    hi = x.astype(BF16)
    lo = (x - hi.astype(F32)).astype(BF16)
    r = _dot(tri, jnp.concatenate([hi, lo], axis=1))
    return r[:, 0:LANES] + r[:, LANES:2 * LANES]


DN_HEADS_PER_STEP = 2


def _dn_kernel(alog_ref, dtb_ref, ng_ref, cw_ref_q, cw_ref_k, cw_ref_v,
                ql_ref, kl_ref, vl_ref, gfl_ref, gbl_ref, gl_ref,
                qc_ref, kc_ref, vc_ref, gfc_ref, gbc_ref, gc_ref,
                yl_ref, *rest, with_ctx_out):
    if with_ctx_out:
        yc_ref, rest = rest[0], rest[1:]
    else:
        yc_ref = None
    q_scr, k_scr, v_scr, g_scr, xp_scr, sf_scr, sb_scr, yacc_scr = rest[:8]
    fwd_bufs, bwd_bufs = rest[8:12], rest[12:16]
    c = DN_CHUNK
    nh = DN_HEADS
    hp = DN_HEADS_PER_STEP
    l = ql_ref.shape[1]
    lc = qc_ref.shape[1]
    dk = ql_ref.shape[2] // hp
    hg = pl.program_id(1)
    kw = DN_CONV_W
    cpad = 8
    n_c, n_l = lc // c, l // c
    nck = n_c + n_l
    heads =[(p, slice(p * dk, (p + 1) * dk)) for p in range(hp)]

    def conv_seq(src_ref, cw_ref, dst_scr, base, n, norm, scale):
        xp_scr[0:cpad, :] = jnp.zeros((cpad, hp * dk), F32)
        xp_scr[cpad:cpad + n, :] = src_ref[0].astype(F32)
        xp_scr[cpad + n:cpad + n + cpad, :] = jnp.zeros((cpad, hp * dk), F32)
        rb = _pick_tile(n, 128)
        for start in range(0, n, rb):
            y = None
            for t in range(kw):
                o = start + cpad - kw // 2 + t
                term = xp_scr[o:o + rb, :] * cw_ref[t:t + 1, :]
                y = term if y is None else y + term
            y = _silu(y)
            if norm:
                y = jnp.concatenate([y[:, hs] * (lax.rsqrt(jnp.sum(y[:, hs] * y[:, hs], axis=-1, keepdims=True) + EPS)
                                                 * scale) for _, hs in heads], axis=1)
            dst_scr[base + start:base + start + rb, :] = y.astype(BF16)

    for src_c, src_l, cw, dst, norm, scale in ((qc_ref, ql_ref, cw_ref_q, q_scr, True, dk ** -0.5),
                                               (kc_ref, kl_ref, cw_ref_k, k_scr, True, 1.0),
                                               (vc_ref, vl_ref, cw_ref_v, v_scr, False, 1.0)):
        conv_seq(src_c, cw, dst, 0, lc, norm, scale)
        conv_seq(src_l, cw, dst, lc, l, norm, scale)
    g_scr[0:lc, :] = gc_ref[0]
    g_scr[lc:lc + l, :] = gl_ref[0]

    lane = lax.broadcasted_iota(jnp.int32, (1, LANES), 1)
    neg_a = -jnp.exp(alog_ref[...])
    dtb = dtb_ref[...]
    ii = lax.broadcasted_iota(jnp.int32, (c, c), 0)
    jj = lax.broadcasted_iota(jnp.int32, (c, c), 1)
    tri_l = (ii >= jj).astype(BF16)
    tri_u = (ii <= jj).astype(BF16)

    def lane_col(x, idx):
        return jnp.broadcast_to(jnp.sum(jnp.where(lane == idx, x, 0.0), axis=1, keepdims=True), (c, LANES))

    def prep_group(cis):
        chains = []
        for ci in cis:
            srows = pl.ds(pl.multiple_of(ci * c, c), c)
            graw = g_scr[srows, :]
            sig = jax.nn.sigmoid(graw)
            la_all = neg_a * jax.nn.softplus(graw + dtb)
            for p, hs in heads:
                h = hg * hp + p
                q, k = q_scr[srows, hs], k_scr[srows, hs]
                qf, kf, vf = q.astype(F32), k.astype(F32), v_scr[srows, hs].astype(F32)
                kk, qk = _dot_nt(k, k), _dot_nt(q, k)
                for upper, bufs in ((False, fwd_bufs), (True, bwd_bufs)):
                    beta = lane_col(sig, h + (nh if upper else 0))
                    la = lane_col(la_all, h + (3 * nh if upper else 2 * nh))
                    chains.append(dict(upper=upper, bufs=bufs, ci=ci, p=p, hs=hs, srows=srows, qf=qf, kf=kf, vf=vf,
                                       kk=kk, qk=qk, beta=beta, la=la))
        for ch in chains:
            ch["g"] = _cumsum_rows(tri_u if ch["upper"] else tri_l, ch["la"])
        for ch in chains:
            g = ch["g"]
            keep = (ii <= jj) if ch["upper"] else (ii >= jj)
            g_row = g.T[0:c, 0:c]
            ch["decay"] = jnp.where(keep, jnp.exp(jnp.where(keep, g[:, 0:c] - g_row, 0.0)), 0.0)
        t_invs = _unit_tri_inverses([jnp.where(ii != jj, ch["kk"] * ch["beta"][:, 0:c] * ch["decay"], 0.0)
                                     for ch in chains])
        sols = []
        for ch, t_inv in zip(chains, t_invs):
            ch["eg"] = jnp.exp(ch["g"])
            rhs = jnp.concatenate([ch["vf"] * ch["beta"], ch["kf"] * (ch["beta"] * ch["eg"])], axis=1).astype(BF16)
            sols.append(_dot(t_inv.astype(BF16), rhs))
        for ch, sol in zip(chains, sols):
            g = ch["g"]
            ch["g_last"] = g[0:1, :] if ch["upper"] else g[c - 1:c, :]
            ch["uw"] = sol.astype(BF16)
            ch["kgt"] = (ch["kf"] * jnp.exp(ch["g_last"] - g)).T.astype(BF16)
            ch["attn"] = (ch["qk"] * ch["decay"]).astype(BF16)
        kus = [_dot(ch["kgt"], ch["uw"]) for ch in chains]
        aus = [_dot(ch["attn"], ch["uw"]) for ch in chains]
        for ch, ku, au in zip(chains, kus, aus):
            mq_scr, n_scr, o_scr, gl_scr = ch["bufs"]
            ci, p, hs = ch["ci"], ch["p"], ch["hs"]
            mq_scr[pl.ds(pl.multiple_of(ci * (dk + c), c), dk), hs] = ku[:, dk:2 * dk].astype(BF16)
            mq_scr[pl.ds(pl.multiple_of(ci * (dk + c) + dk, c), c), hs] = (ch["qf"] * ch["eg"] - au[:, dk:2 * dk]).astype(BF16)
            n_scr[pl.ds(pl.multiple_of(ci * dk, dk), dk), hs] = ku[:, 0:dk].astype(BF16)
            o_scr[ch["srows"], hs] = au[:, 0:dk]
            gl_scr[p, pl.ds(pl.multiple_of(ci * 8, 8), 8), :] = jnp.broadcast_to(jnp.exp(ch["g_last"]), (8, LANES))

    nb = next(t for t in (9, 6, 4, 2, 1) if nck % t == 0)

    def prep_body(r, carry):
        prep_group([r * nb + t for t in range(nb)])
        return carry

    lax.fori_loop(0, nck // nb, prep_body, 0)

    sf_scr[...] = jnp.zeros_like(sf_scr)
    sb_scr[...] = jnp.zeros_like(sb_scr)
    norm_g = ng_ref[...]

    def step_all(base, chunk_ids, gate_refs):
        chains = []
        for ci, gate_ref, s_scr, bufs in zip(chunk_ids, gate_refs, (sf_scr, sb_scr), (fwd_bufs, bwd_bufs)):
            gci = base // c + ci
            for p, hs in heads:
                chains.append(dict(rows=pl.ds(pl.multiple_of(ci * c, c), c), gci=gci, gate_ref=gate_ref, s_scr=s_scr,
                                   bufs=bufs, p=p, hs=hs, srows=pl.ds(pl.multiple_of(gci * c, c), c)))
        for ch in chains:
            mq_scr, n_scr, o_scr, gl_scr = ch["bufs"]
            ch["s"] = ch["s_scr"][ch["p"]]
            mq = mq_scr[pl.ds(pl.multiple_of(ch["gci"] * (dk + c), c), dk + c), ch["hs"]]
            ch["r"] = _dot(mq, ch["s"].astype(BF16))
        for ch in chains:
            mq_scr, n_scr, o_scr, gl_scr = ch["bufs"]
            gl = gl_scr[ch["p"], pl.ds(pl.multiple_of(ch["gci"] * 8, 8), 8), :]
            incr = n_scr[pl.ds(pl.multiple_of(ch["gci"] * dk, dk), dk), ch["hs"]].astype(F32)
            ch["s_scr"][ch["p"]] = ch["s"] * gl[0:1, :] + (incr - ch["r"][0:dk, :])
        outs = []
        for ch in chains:
            if ch["gate_ref"] is None:
                outs.append(None)
                continue
            o = ch["r"][dk:dk + c, :] + ch["bufs"][2][ch["srows"], ch["hs"]]
            y = o * lax.rsqrt(jnp.mean(o * o, axis=-1, keepdims=True) + EPS) * norm_g
            outs.append(y * _silu(ch["gate_ref"][0, ch["rows"], ch["hs"]].astype(F32)))
        return chains, outs

    def make_step(base, gf_ref, gb_ref, y_ref):
        def step(cf, cb, second_half):
            chains, outs = step_all(base, (cf, cb), (gf_ref, gb_ref))
            if second_half is None:
                return
            for ch, y in zip(chains, outs):
                if second_half:
                    y_ref[0, ch["rows"], ch["hs"]] = (yacc_scr[ch["rows"], ch["hs"]] + y).astype(y_ref.dtype)
                else:
                    yacc_scr[ch["rows"], ch["hs"]] = y
        return step

    if with_ctx_out:
        _run_bidirectional(n_c, make_step(0, gfc_ref, gbc_ref, yc_ref), True)
    else:
        _run_bidirectional(n_c, make_step(0, None, None, None), False)
    _run_bidirectional(n_l, make_step(lc, gfl_ref, gbl_ref, yl_ref), True)


def _deltanet_core(p_lat, g_lat, p_ctx, g_ctx, conv_w, a_lanes, dtb_lanes, norm_g, with_ctx_out):
    b, l, _ = p_lat.shape
    lc = p_ctx.shape[1]
    nh = DN_HEADS
    hp = DN_HEADS_PER_STEP
    ng = nh // hp
    dk = norm_g.shape[-1]
    w = hp * dk

    def specs(n):
        sp = [pl.BlockSpec((1, n, w), lambda bi, h, k=k: (bi, 0, k * ng + h)) for k in range(5)]
        return sp + [pl.BlockSpec((1, n, LANES), lambda bi, h: (bi, 0, 0))]

    const2 = lambda bi, h: (0, 0)
    kw = conv_w.shape[0]
    c = DN_CHUNK
    ltot = lc + l
    nck = ltot // c
    dir_bufs = [pltpu.VMEM((nck * (dk + c), w), BF16), pltpu.VMEM((nck * dk, w), BF16), pltpu.VMEM((ltot, w), F32),
                pltpu.VMEM((hp, nck * 8, LANES), F32)]
    out_specs = [pl.BlockSpec((1, l, w), lambda bi, h: (bi, 0, h))]
    out_shape = [jax.ShapeDtypeStruct((b, l, nh * dk), BF16)]
    if with_ctx_out:
        out_specs.append(pl.BlockSpec((1, lc, w), lambda bi, h: (bi, 0, h)))
        out_shape.append(jax.ShapeDtypeStruct((b, lc, nh * dk), BF16))
    outs = pl.pallas_call(
        functools.partial(_dn_kernel, with_ctx_out=with_ctx_out),
        grid=(b, ng),
        in_specs=[pl.BlockSpec((1, LANES), const2), pl.BlockSpec((1, LANES), const2), pl.BlockSpec((1, dk), const2)]
                 + [pl.BlockSpec((kw, w), lambda bi, h, k=k: (0, k * ng + h)) for k in range(3)]
                 + specs(l) + specs(lc),
        out_specs=out_specs, out_shape=out_shape,
        scratch_shapes=[pltpu.VMEM((ltot, w), BF16), pltpu.VMEM((ltot, w), BF16), pltpu.VMEM((ltot, w), BF16),
                        pltpu.VMEM((ltot, LANES), F32), pltpu.VMEM((max(l, lc) + 16, w), F32),
                        pltpu.VMEM((hp, dk, dk), F32), pltpu.VMEM((hp, dk, dk), F32),
                        pltpu.VMEM((max(l, lc), w), F32)] + 2 * dir_bufs,
        compiler_params=pltpu.CompilerParams(dimension_semantics=("parallel", "parallel"),
                                             vmem_limit_bytes=DN_VMEM_LIMIT),
        name="deltanet_core",
    )(a_lanes, dtb_lanes, norm_g.reshape(1, dk), conv_w, conv_w, conv_w,
      *([p_lat] * 5), g_lat, *([p_ctx] * 5), g_ctx)
    return (outs[0], outs[1]) if with_ctx_out else (outs[0], None)


def kernel(x, c, ctx, c_ctx, ada_w, ada_b, mix_pre_g, mix_post_g, mlp_pre_g, mlp_post_g, mlp_w1, mlp_w2, ret_w_in, ret_decay_logit, ret_w_out, att_w_in, att_sink, att_w_out, pool_w, pool_b, pool_scale, dn_w_in, dn_conv_w, dn_a_log, dn_dt_bias, dn_norm_g, dn_w_out):
    b, l, d = x.shape
    lc = ctx.shape[1]
    depth = ada_w.shape[0]
    n_mixers = 4

    rows = -(-(b + 1) // 8) * 8
    cond = jnp.zeros((rows, d), F32).at[:b].set(c).at[b].set(c_ctx)
    mods = _modulation(cond, ada_w, ada_b)
    xc = ctx.reshape(1, b * lc, d)

    for i in range(depth):
        kind, inst = i % n_mixers, i // n_mixers
        need_ctx = i < depth - 1
        m_lat = mods[i, :b].reshape(b, N_MOD, d)
        m_ctx = mods[i, b:b + 1].reshape(1, N_MOD, d)
        w1, w2 = mlp_w1[i].astype(BF16), mlp_w2[i].astype(BF16)
        post_kw = {}
        if kind == 0:
            w_in = ret_w_in[inst].astype(BF16)
            p_lat = _inproj(x, m_lat, mix_pre_g[i], w_in, name="ret_inproj")
            p_ctx = _inproj(xc, m_ctx, mix_pre_g[i], w_in, name="ret_inproj_ctx").reshape(b, lc, -1)
            dk = d // RET_HEADS
            cos, sin = _rope_tables(l, dk, 1)
            a_lat, a_ctx = _retention_core(p_lat, p_ctx, ret_decay_logit[inst], cos, sin)
            w_out = ret_w_out[inst].astype(BF16)
        elif kind == 1:
            grp = ATT_HEADS // ATT_KV_HEADS
            dh = d // ATT_HEADS
            nq, nkv = ATT_HEADS * dh, ATT_KV_HEADS * dh
            wq = att_w_in[inst][:, :nq].reshape(d, ATT_KV_HEADS, grp * dh)
            wk = att_w_in[inst][:, nq:nq + nkv].reshape(d, ATT_KV_HEADS, dh)
            wv = att_w_in[inst][:, nq + nkv:].reshape(d, ATT_KV_HEADS, dh)
            w_in = jnp.concatenate([wq, wk, wv], axis=2).reshape(d, nq + 2 * nkv).astype(BF16)
            p_lat = _inproj(x, m_lat, mix_pre_g[i], w_in, name="att_inproj")
            p_ctx = _inproj(xc, m_ctx, mix_pre_g[i], w_in, name="att_inproj_ctx").reshape(b, lc, -1)
            cos_h, sin_h = _rope_tables(l, dh, 1)
            one, zero = jnp.ones((l, dh), F32), jnp.zeros((l, dh), F32)
            qs = dh ** -0.5
            cos = jnp.concatenate([jnp.tile(cos_h, (1, grp)) * qs, cos_h, one], axis=1)
            sin = jnp.concatenate([jnp.tile(sin_h, (1, grp)) * qs, sin_h, zero], axis=1)
            a_lat, a_ctx = _attention_core(p_lat, p_ctx, att_sink[inst], cos, sin, need_ctx)
            w_out = att_w_out[inst].astype(BF16)
        elif kind == 2:
            a_lat = _pool_core(x, m_lat, mix_pre_g[i])
            a_ctx = _pool_core(xc.reshape(b, lc, d), jnp.broadcast_to(m_ctx, (b, N_MOD, d)), mix_pre_g[i]) if need_ctx else None
            w_out = pool_w[inst].astype(BF16)
            post_kw = dict(pool_bias=pool_b[inst], pool_scale=pool_scale[inst])
        else:
            nh = DN_HEADS
            dk = d // nh
            nqkv = 3 * nh * dk
            wd = dn_w_in[inst]
            w_main = jnp.concatenate([wd[:, :nqkv], wd[:, nqkv + 4 * nh:]], axis=1).astype(BF16)
            w_gate = jnp.zeros((d, LANES), F32).at[:, :4 * nh].set(wd[:, nqkv:nqkv + 4 * nh]).astype(BF16)
            p_lat, g_lat = _inproj(x, m_lat, mix_pre_g[i], w_main, aux_w=w_gate, name="dn_inproj")
            p_ctx, g_ctx = _inproj(xc, m_ctx, mix_pre_g[i], w_main, aux_w=w_gate, name="dn_inproj_ctx")
            p_ctx, g_ctx = p_ctx.reshape(b, lc, -1), g_ctx.reshape(b, lc, -1)
            a_lanes = jnp.zeros((1, LANES), F32).at[0, 2 * nh:4 * nh].set(dn_a_log[inst].reshape(-1))
            dtb_lanes = jnp.zeros((1, LANES), F32).at[0, 2 * nh:4 * nh].set(dn_dt_bias[inst].reshape(-1))
            a_lat, a_ctx = _deltanet_core(p_lat, g_lat, p_ctx, g_ctx, dn_conv_w[inst], a_lanes, dtb_lanes,
                                          dn_norm_g[inst], need_ctx)
            w_out = dn_w_out[inst].astype(BF16)

        x = _post(a_lat, x, m_lat, mix_post_g[i], mlp_pre_g[i], mlp_post_g[i], w_out, w1, w2,
                  name="post", **post_kw)
        if need_ctx:
            xc = _post(a_ctx.reshape(1, b * lc, -1), xc, m_ctx, mix_post_g[i], mlp_pre_g[i], mlp_post_g[i],
                       w_out, w1, w2, name="post_ctx", **post_kw)
    return x
```

```python
import functools

import jax
import jax.numpy as jnp
from jax import lax
from jax.experimental import pallas as pl
from jax.experimental.pallas import tpu as pltpu

F32 = jnp.float32
BF16 = jnp.bfloat16
HI = lax.Precision.HIGHEST

EPS = 1e-6
NEG_INF = -1e30
ROPE_BASE = 10000.0
GRID_W = 64
N_MOD = 6
RET_HEADS = 8
RET_CHUNK = 128
ATT_HEADS = 16
ATT_KV_HEADS = 4
ATT_WINDOW = 128
ATT_BLOCK = 128
POOL_WINDOWS = (2, 4, 8, 16)
POOL_PAD = 16
DN_HEADS = 8
DN_CHUNK = 128
DN_CONV_W = 5

LANES = 128
VMEM_LIMIT = 48 * 1024 * 1024
DN_VMEM_LIMIT = 56 * 1024 * 1024
INPROJ_MAX_TN = 2560


def _cparams(sem):
    return pltpu.CompilerParams(dimension_semantics=sem, vmem_limit_bytes=VMEM_LIMIT)


def _dot(a, b):
    return jnp.dot(a, b, preferred_element_type=F32)


def _dot_nt(a, b):
    return lax.dot_general(a, b, (((1,), (1,)), ((), ())), preferred_element_type=F32)


def _dot_hi(a, b):
    return jnp.dot(a, b, preferred_element_type=F32, precision=HI)


def _rms(x):
    return x * lax.rsqrt(jnp.mean(x * x, axis=-1, keepdims=True) + EPS)


def _silu(x):
    return x * jax.nn.sigmoid(x)


def _pick_tile(n, pref):
    t = min(pref, n)
    while n % t:
        t //= 2
    return t


def _mod_kernel(c_ref, w_ref, b_ref, o_ref):
    o_ref[0] = _dot_hi(_silu(c_ref[...]), w_ref[0]) + b_ref[0]


def _modulation(cond, ada_w, ada_b):
    depth, d, n = ada_w.shape
    rows = cond.shape[0]
    tn = _pick_tile(n, 1536)
    return pl.pallas_call(
        _mod_kernel,
        grid=(depth, n // tn),
        in_specs=[pl.BlockSpec((rows, d), lambda i, j: (0, 0)),
                  pl.BlockSpec((1, d, tn), lambda i, j: (i, 0, j)),
                  pl.BlockSpec((1, 1, tn), lambda i, j: (i, 0, j))],
        out_specs=pl.BlockSpec((1, rows, tn), lambda i, j: (i, 0, j)),
        out_shape=jax.ShapeDtypeStruct((depth, rows, n), F32),
        compiler_params=_cparams(("parallel", "parallel")),
        name="modulation",
    )(cond, ada_w, ada_b.reshape(depth, 1, n))


def _inproj_kernel(x_ref, mod_ref, g_ref, w_ref, *rest, has_aux):
    if has_aux:
        wa_ref, o_ref, oa_ref, u_scr = rest
    else:
        o_ref, u_scr = rest

    @pl.when(pl.program_id(2) == 0)
    def _():
        m = mod_ref[0]
        u = _rms(x_ref[0]) * g_ref[...] * (1.0 + m[1:2]) + m[0:1]
        u_scr[...] = u.astype(BF16)
        if has_aux:
            oa_ref[0] = _dot(u_scr[...], wa_ref[...])

    o_ref[0] = _dot(u_scr[...], w_ref[...]).astype(o_ref.dtype)


def _inproj(x, mod, g, w, aux_w=None, name="inproj"):
    bx, lx, d = x.shape
    n = w.shape[1]
    tm = _pick_tile(lx, 1024)
    tn = max(t for t in range(LANES, min(n, INPROJ_MAX_TN) + 1, LANES) if n % t == 0)
    has_aux = aux_w is not None
    in_specs = [pl.BlockSpec((1, tm, d), lambda b, i, j: (b, i, 0)),
                pl.BlockSpec((1, N_MOD, d), lambda b, i, j: (b, 0, 0)),
                pl.BlockSpec((1, d), lambda b, i, j: (0, 0)),
                pl.BlockSpec((d, tn), lambda b, i, j: (0, j))]
    out_specs = [pl.BlockSpec((1, tm, tn), lambda b, i, j: (b, i, j))]
    out_shape = [jax.ShapeDtypeStruct((bx, lx, n), BF16)]
    args = [x, mod, g.reshape(1, d), w]
    if has_aux:
        na = aux_w.shape[1]
        in_specs.append(pl.BlockSpec((d, na), lambda b, i, j: (0, 0)))
        out_specs.append(pl.BlockSpec((1, tm, na), lambda b, i, j: (b, i, 0)))
        out_shape.append(jax.ShapeDtypeStruct((bx, lx, na), F32))
        args.append(aux_w)
    outs = pl.pallas_call(
        functools.partial(_inproj_kernel, has_aux=has_aux),
        grid=(bx, lx // tm, n // tn),
        in_specs=in_specs, out_specs=out_specs, out_shape=out_shape,
        scratch_shapes=[pltpu.VMEM((tm, d), BF16)],
        compiler_params=_cparams(("parallel", "parallel", "arbitrary")),
        name=name,
    )(*args)
    return outs if has_aux else outs[0]


def _post_kernel(a_ref, x_ref, mod_ref, gpost_ref, gpre2_ref, gpost2_ref, wo_ref, *rest, grouped, tf):
    if grouped:
        pb_ref, ps_ref, w1_ref, w2_ref, o_ref = rest
    else:
        w1_ref, w2_ref, o_ref = rest
    a = a_ref[0]
    if grouped:
        ng, gw, _ = wo_ref.shape
        y = jnp.concatenate([_dot(a[:, g * gw:(g + 1) * gw], wo_ref[g]) for g in range(ng)], axis=1)
        y = (y + pb_ref[...]) * ps_ref[...]
    else:
        y = _dot(a, wo_ref[...])
    m = mod_ref[0]
    x1 = x_ref[0] + m[2:3] * (_rms(y) * gpost_ref[...])
    u = (_rms(x1) * gpre2_ref[...] * (1.0 + m[4:5]) + m[3:4]).astype(BF16)
    acc = None
    for j in range(w1_ref.shape[1] // tf):
        h = jnp.square(jnp.maximum(_dot(u, w1_ref[:, j * tf:(j + 1) * tf]), 0.0)).astype(BF16)
        t = _dot(h, w2_ref[j * tf:(j + 1) * tf, :])
        acc = t if acc is None else acc + t
    o_ref[0] = x1 + m[5:6] * (_rms(acc) * gpost2_ref[...])


def _post(a, x, mod, gpost, gpre2, gpost2, w_out, w1, w2, pool_bias=None, pool_scale=None, name="post"):
    bx, lx, d = x.shape
    ka = a.shape[2]
    dff = w1.shape[1]
    tm = _pick_tile(lx, 512)
    tf = _pick_tile(dff, 512)
    grouped = pool_bias is not None
    row = lambda v: v.reshape(1, d)
    const2 = lambda b, i: (0, 0)
    once = pl.Buffered(1)
    in_specs = [pl.BlockSpec((1, tm, ka), lambda b, i: (b, i, 0)),
                pl.BlockSpec((1, tm, d), lambda b, i: (b, i, 0)),
                pl.BlockSpec((1, N_MOD, d), lambda b, i: (b, 0, 0)),
                pl.BlockSpec((1, d), const2), pl.BlockSpec((1, d), const2), pl.BlockSpec((1, d), const2)]
    args = [a, x, mod, row(gpost), row(gpre2), row(gpost2), w_out]
    if grouped:
        in_specs += [pl.BlockSpec(w_out.shape, lambda b, i: (0, 0, 0), pipeline_mode=once),
                     pl.BlockSpec((1, d), const2), pl.BlockSpec((1, d), const2)]
        args += [row(pool_bias), row(pool_scale)]
    else:
        in_specs += [pl.BlockSpec(w_out.shape, const2, pipeline_mode=once)]
    in_specs += [pl.BlockSpec((d, dff), const2, pipeline_mode=once), pl.BlockSpec((dff, d), const2, pipeline_mode=once)]
    args += [w1, w2]
    return pl.pallas_call(
        functools.partial(_post_kernel, grouped=grouped, tf=tf),
        grid=(bx, lx // tm),
        in_specs=in_specs,
        out_specs=pl.BlockSpec((1, tm, d), lambda b, i: (b, i, 0)),
        out_shape=jax.ShapeDtypeStruct((bx, lx, d), F32),
        compiler_params=_cparams(("parallel", "parallel")),
        name=name,
    )(*args)


def _rope_tables(n, head_dim, n_rep, scale=1.0):
    quarter = head_dim // 4
    inv = ROPE_BASE ** (-jnp.arange(quarter, dtype=F32) / quarter)
    t = jnp.arange(n, dtype=jnp.int32)
    ang_r = (t // GRID_W).astype(F32)[:, None] * inv[None, :]
    ang_c = (t % GRID_W).astype(F32)[:, None] * inv[None, :]
    cos = jnp.concatenate([jnp.cos(ang_r), jnp.cos(ang_r), jnp.cos(ang_c), jnp.cos(ang_c)], axis=1)
    sin = jnp.concatenate([-jnp.sin(ang_r), jnp.sin(ang_r), -jnp.sin(ang_c), jnp.sin(ang_c)], axis=1)
    return jnp.tile(cos, (1, n_rep)) * scale, jnp.tile(sin, (1, n_rep)) * scale


def _rope(x, cos, sin, quarter):
    r = lax.broadcasted_iota(jnp.int32, (LANES, LANES), 0)
    col = lax.broadcasted_iota(jnp.int32, (LANES, LANES), 1)
    partner = jnp.where((col & (2 * quarter - 1)) < quarter, col + quarter, col - quarter)
    perm = (r == partner).astype(BF16)
    swapped = jnp.concatenate([_dot(x[:, g * LANES:(g + 1) * LANES], perm) for g in range(x.shape[1] // LANES)], axis=1)
    return x.astype(F32) * cos + swapped * sin


def _head_scalar(vec, h):
    lane = lax.broadcasted_iota(jnp.int32, vec.shape, 1)
    return jnp.sum(jnp.where(lane == h, vec, 0.0), axis=1, keepdims=True)


def _log_sigmoid(x):
    return jnp.minimum(x, 0.0) - jnp.log1p(jnp.exp(-jnp.abs(x)))


def _run_bidirectional(n, step, want_out):
    unroll = next(u for u in (8, 4, 2, 1) if n % (2 * u) == 0)
    if want_out:
        assert n % 2 == 0
        lax.fori_loop(0, n // 2, lambda i, c: step(i, n - 1 - i, False) or c, 0, unroll=unroll)
        lax.fori_loop(n // 2, n, lambda i, c: step(i, n - 1 - i, True) or c, 0, unroll=unroll)
    else:
        lax.fori_loop(0, n, lambda i, c: step(i, n - 1 - i, None) or c, 0, unroll=unroll)


def _ret_kernel(dl_ref, cos_ref, sin_ref, ql_ref, kl_ref, vl_ref, gfl_ref, gbl_ref,
                qc_ref, kc_ref, vc_ref, gfc_ref, gbc_ref, yl_ref, yc_ref,
                q_scr, k_scr, v_scr, p_scr, s_all_scr):
    c = RET_CHUNK
    l = ql_ref.shape[1]
    lc = qc_ref.shape[1]
    dk = ql_ref.shape[2]
    h = pl.program_id(1)
    scale = dk ** -0.5

    ls = _log_sigmoid(dl_ref[...])
    lgf = _head_scalar(ls[0:1], h)
    lgb = _head_scalar(ls[1:2], h)

    q_scr[0:lc, :] = (qc_ref[0].astype(F32) * scale).astype(BF16)
    k_scr[0:lc, :] = kc_ref[0]
    rb = _pick_tile(l, 256)

    def rope_body(r, carry):
        rows = pl.ds(pl.multiple_of(r * rb, rb), rb)
        dst = pl.ds(pl.multiple_of(lc + r * rb, rb), rb)
        cs, sn = cos_ref[rows, :], sin_ref[rows, :]
        q_scr[dst, :] = (_rope(ql_ref[0, rows, :], cs, sn, dk // 4) * scale).astype(BF16)
        k_scr[dst, :] = _rope(kl_ref[0, rows, :], cs, sn, dk // 4).astype(BF16)
        return carry

    lax.fori_loop(0, l // rb, rope_body, 0, unroll=2 - (l // rb) % 2)

    ii = lax.broadcasted_iota(jnp.int32, (c, c), 0).astype(F32)
    jj = lax.broadcasted_iota(jnp.int32, (c, c), 1).astype(F32)
    diff = ii - jj
    dec_f = jnp.where(diff >= 0, jnp.exp(lgf * jnp.maximum(diff, 0.0)), 0.0)
    dec_b = jnp.where(diff <= 0, jnp.exp(lgb * jnp.maximum(-diff, 0.0)), 0.0)
    idx = lax.broadcasted_iota(jnp.int32, (c, 1), 0).astype(F32)
    dv = vl_ref.shape[2]
    qdec_f, cd_f = jnp.exp(lgf * (idx + 1.0)), jnp.exp(lgf * c)
    qdec_b, cd_b = jnp.exp(lgb * (c - idx)), jnp.exp(lgb * c)

    n_c, n_l = lc // c, l // c
    n = n_c + n_l
    v_scr[0:lc, :] = vc_ref[0]
    v_scr[lc:lc + l, :] = vl_ref[0]

    lane_idx = lax.broadcasted_iota(jnp.int32, (1, c), 1).astype(F32)
    kdec_f_row, kdec_b_row = jnp.exp(lgf * (c - 1.0 - lane_idx)), jnp.exp(lgb * lane_idx)

    grp = _pick_tile(n_l, 16)

    def incr_group(cis):
        srows = [pl.ds(pl.multiple_of(ci * c, c), c) for ci in cis]
        kts = [k_scr[sr, :].astype(F32).T for sr in srows]
        lhs = [jnp.concatenate([kt * kdec_f_row, kt * kdec_b_row], axis=0).astype(BF16) for kt in kts]
        ps = [_dot(a, v_scr[sr, :]) for a, sr in zip(lhs, srows)]
        for ci, p in zip(cis, ps):
            p_scr[pl.ds(pl.multiple_of(ci * 2 * dk, 2 * dk), 2 * dk), :] = p

    def incr_body(r, carry):
        incr_group([n_c + r * grp + t for t in range(grp)])
        return carry

    incr_group(list(range(n_c)))
    lax.fori_loop(0, n_l // grp, incr_body, 0)

    def scan(first, count, step, s, half, cd):
        def body(t, s):
            ci = first + t * step
            s_all_scr[pl.ds(pl.multiple_of(ci * dk, dk), dk), half * dv:(half + 1) * dv] = s.astype(BF16)
            return s * cd + p_scr[pl.ds(pl.multiple_of((ci * 2 + half) * dk, dk), dk), :]
        return lax.fori_loop(0, count, body, s)

    zero = jnp.zeros((dk, dv), F32)
    scan(0, n, 1, zero, 0, cd_f)
    scan(n - 1, n_l, -1, scan(n_c - 1, n_c, -1, zero, 1, cd_b), 1, cd_b)

    def out_group(base_chunk, cis, gf_ref, gb_ref, y_ref):
        srows = [pl.ds(pl.multiple_of((base_chunk + ci) * c, c), c) for ci in cis]
        qs = [q_scr[sr, :] for sr in srows]
        ss = [_dot_nt(q, k_scr[sr, :]) for q, sr in zip(qs, srows)]
        cross = [_dot(q, s_all_scr[pl.ds(pl.multiple_of((base_chunk + ci) * dk, dk), dk), :]) for q, ci in zip(qs, cis)]
        intra = [_dot(jnp.concatenate([s * dec_f, s * dec_b], axis=0).astype(BF16), v_scr[sr, :])
                 for s, sr in zip(ss, srows)]
        for ci, it, cr in zip(cis, intra, cross):
            rows = pl.ds(pl.multiple_of(ci * c, c), c)
            y = None
            for half, (qdec, g_ref) in enumerate(((qdec_f, gf_ref), (qdec_b, gb_ref))):
                o = it[half * c:(half + 1) * c, :] + qdec * cr[:, half * dv:(half + 1) * dv]
                t = o * lax.rsqrt(jnp.mean(o * o, axis=-1, keepdims=True) + EPS) * _silu(g_ref[0, rows, :].astype(F32))
                y = t if y is None else y + t
            y_ref[0, rows, :] = y.astype(y_ref.dtype)

    out_group(0, list(range(n_c)), gfc_ref, gbc_ref, yc_ref)

    def out_body(r, carry):
        out_group(n_c, [r * grp + t for t in range(grp)], gfl_ref, gbl_ref, yl_ref)
        return carry

    lax.fori_loop(0, n_l // grp, out_body, 0)


def _retention_core(p_lat, p_ctx, decay_logit, cos, sin):
    b, l, _ = p_lat.shape
    lc = p_ctx.shape[1]
    nh = RET_HEADS
    dk = cos.shape[1]
    dv = 2 * dk

    def specs(n):
        return [pl.BlockSpec((1, n, dk), lambda bi, h: (bi, 0, h)),
                pl.BlockSpec((1, n, dk), lambda bi, h: (bi, 0, nh + h)),
                pl.BlockSpec((1, n, dv), lambda bi, h: (bi, 0, nh + h)),
                pl.BlockSpec((1, n, dv), lambda bi, h: (bi, 0, 2 * nh + h)),
                pl.BlockSpec((1, n, dv), lambda bi, h: (bi, 0, 3 * nh + h))]

    const2 = lambda bi, h: (0, 0)
    return pl.pallas_call(
        _ret_kernel,
        grid=(b, nh),
        in_specs=[pl.BlockSpec(decay_logit.shape, const2), pl.BlockSpec((l, dk), const2),
                  pl.BlockSpec((l, dk), const2)] + specs(l) + specs(lc),
        out_specs=[pl.BlockSpec((1, l, dv), lambda bi, h: (bi, 0, h)),
                   pl.BlockSpec((1, lc, dv), lambda bi, h: (bi, 0, h))],
        out_shape=[jax.ShapeDtypeStruct((b, l, nh * dv), BF16), jax.ShapeDtypeStruct((b, lc, nh * dv), BF16)],
        scratch_shapes=[pltpu.VMEM((lc + l, dk), BF16), pltpu.VMEM((lc + l, dk), BF16),
                        pltpu.VMEM((lc + l, dv), BF16),
                        pltpu.VMEM(((lc + l) // RET_CHUNK * 2 * dk, dv), F32),
                        pltpu.VMEM(((lc + l) // RET_CHUNK * dk, 2 * dv), BF16)],
        compiler_params=_cparams(("parallel", "parallel")),
        name="retention_core",
    )(decay_logit, cos, sin, *([p_lat] * 5), *([p_ctx] * 5))


def _att_kernel(sink_ref, cos_ref, sin_ref, pl_ref, pc_ref, ol_ref, *rest, with_ctx_out):
    if with_ctx_out:
        oc_ref, rest = rest[0], rest[1:]
    q_scr, k_scr, v_scr, kc_scr, vc_scr, bias_scr = rest
    grp = ATT_HEADS // ATT_KV_HEADS
    qb, win = ATT_BLOCK, ATT_WINDOW
    span = qb + 2 * win
    l = pl_ref.shape[1]
    lc = pc_ref.shape[1]
    dh = pl_ref.shape[2] // (grp + 2)
    nq = grp * dh
    nblk = l // qb
    npair = nq // LANES
    kh = pl.program_id(1)
    assert 2 * dh == LANES

    def split_kv(kv, k_dst, v_dst, rows):
        lo = lax.broadcasted_iota(jnp.int32, kv.shape, 1) < dh
        vk = pltpu.roll(kv, dh, 1)
        one = jnp.ones_like(kv)
        k_dst[0, rows, :] = jnp.where(lo, kv, 0.0).astype(BF16)
        k_dst[1, rows, :] = jnp.where(lo, 0.0, vk).astype(BF16)
        v_dst[0, rows, :] = jnp.concatenate([jnp.where(lo, vk, 0.0), jnp.where(lo, one, 0.0)], axis=1).astype(BF16)
        v_dst[1, rows, :] = jnp.concatenate([jnp.where(lo, 0.0, kv), jnp.where(lo, 0.0, one)], axis=1).astype(BF16)

    rb = _pick_tile(l, 256)

    def rope_body(r, carry):
        rows = pl.ds(pl.multiple_of(r * rb, rb), rb)
        x = _rope(pl_ref[0, rows, :], cos_ref[rows, :], sin_ref[rows, :], dh // 4)
        q_scr[rows, :] = x[:, 0:nq].astype(BF16)
        split_kv(x[:, nq:nq + 2 * dh], k_scr, v_scr, rows)
        return carry

    lax.fori_loop(0, l // rb, rope_body, 0, unroll=2 - (l // rb) % 2)
    split_kv(pc_ref[0, :, nq:nq + 2 * dh].astype(F32), kc_scr, vc_scr, slice(None))

    qi = lax.broadcasted_iota(jnp.int32, (qb, span), 0)
    kj = lax.broadcasted_iota(jnp.int32, (qb, span), 1)
    for e in range(3):
        bias_scr[e] = jnp.where(jnp.abs(kj - e * win - qi) <= win, 0.0, NEG_INF)
    lo_half = lax.broadcasted_iota(jnp.int32, (1, LANES), 1) < dh

    def attend(problems):
        scores = []
        for q, _, parts in problems:
            scores.append([[_dot_nt(q, kk[par]) if bias is None else _dot_nt(q, kk[par]) + bias
                            for kk, _, bias in parts] for par in range(2)])
        mxs = []
        for (_, pair, _), sc in zip(problems, scores):
            row = []
            for par in range(2):
                mx = jnp.full((sc[par][0].shape[0], 1), sink_ref[kh * grp + 2 * pair + par], F32)
                for s in sc[par]:
                    mx = jnp.maximum(mx, jnp.max(s, axis=-1, keepdims=True))
                row.append(mx)
            mxs.append(row)
        outs = []
        for (_, pair, parts), sc, mx in zip(problems, scores, mxs):
            acc = None
            for par in range(2):
                for s, (_, vv, _) in zip(sc[par], parts):
                    t = _dot(jnp.exp(s - mx[par]).astype(BF16), vv[par])
                    acc = t if acc is None else acc + t
            sink_e = jnp.exp(sink_ref[kh * grp + 2 * pair] - mx[0])
            sink_o = jnp.exp(sink_ref[kh * grp + 2 * pair + 1] - mx[1])
            outs.append(acc[:, 0:LANES] / (acc[:, LANES:2 * LANES] + jnp.where(lo_half, sink_e, sink_o)))
        return outs

    kc_parts = ([kc_scr[0], kc_scr[1]], [vc_scr[0], vc_scr[1]])
    nb = _pick_tile(nblk, 8)

    def block_group(r, carry):
        problems, dests = [], []
        for t in range(nb):
            bi = r * nb + t
            qs = pl.multiple_of(bi * qb, qb)
            ks = pl.multiple_of(jnp.clip(qs - win, 0, l - span), qb)
            edge = jnp.where(bi == 0, 0, jnp.where(bi == nblk - 1, 2, 1))
            bias = bias_scr[edge]
            kwin = [k_scr[par, pl.ds(ks, span), :] for par in range(2)]
            vwin = [v_scr[par, pl.ds(ks, span), :] for par in range(2)]
            for pair in range(npair):
                q = q_scr[pl.ds(qs, qb), pair * LANES:(pair + 1) * LANES]
                problems.append((q, pair, [(kwin, vwin, bias), (kc_parts[0], kc_parts[1], None)]))
                dests.append((qs, pair))
        for (qs, pair), o in zip(dests, attend(problems)):
            ol_ref[0, pl.ds(qs, qb), pair * LANES:(pair + 1) * LANES] = o.astype(ol_ref.dtype)
        return carry

    lax.fori_loop(0, nblk // nb, block_group, 0)

    if with_ctx_out:
        problems = []
        for pair in range(npair):
            qc = (pc_ref[0, :, pair * LANES:(pair + 1) * LANES].astype(F32) * dh ** -0.5).astype(BF16)
            problems.append((qc, pair, [(kc_parts[0], kc_parts[1], None)]))
        for pair, o in enumerate(attend(problems)):
            oc_ref[0, :, pair * LANES:(pair + 1) * LANES] = o.astype(oc_ref.dtype)


def _attention_core(p_lat, p_ctx, sink, cos, sin, with_ctx_out):
    b, l, n = p_lat.shape
    lc = p_ctx.shape[1]
    nkv = ATT_KV_HEADS
    wblk = n // nkv
    nq = wblk * (ATT_HEADS // nkv) // (ATT_HEADS // nkv + 2)
    const2 = lambda bi, h: (0, 0)
    out_specs = [pl.BlockSpec((1, l, nq), lambda bi, h: (bi, 0, h))]
    out_shape = [jax.ShapeDtypeStruct((b, l, nkv * nq), BF16)]
    if with_ctx_out:
        out_specs.append(pl.BlockSpec((1, lc, nq), lambda bi, h: (bi, 0, h)))
        out_shape.append(jax.ShapeDtypeStruct((b, lc, nkv * nq), BF16))
    outs = pl.pallas_call(
        functools.partial(_att_kernel, with_ctx_out=with_ctx_out),
        grid=(b, nkv),
        in_specs=[pl.BlockSpec(memory_space=pltpu.SMEM),
                  pl.BlockSpec((l, wblk), const2), pl.BlockSpec((l, wblk), const2),
                  pl.BlockSpec((1, l, wblk), lambda bi, h: (bi, 0, h)),
                  pl.BlockSpec((1, lc, wblk), lambda bi, h: (bi, 0, h))],
        out_specs=out_specs, out_shape=out_shape,
        scratch_shapes=[pltpu.VMEM((l, nq), BF16), pltpu.VMEM((2, l, LANES), BF16), pltpu.VMEM((2, l, 2 * LANES), BF16),
                        pltpu.VMEM((2, lc, LANES), BF16), pltpu.VMEM((2, lc, 2 * LANES), BF16),
                        pltpu.VMEM((3, ATT_BLOCK, ATT_BLOCK + 2 * ATT_WINDOW), F32)],
        compiler_params=_cparams(("parallel", "parallel")),
        name="attention_core",
    )(sink, cos, sin, p_lat, p_ctx)
    return (outs[0], outs[1]) if with_ctx_out else (outs[0], None)


def _pool_kernel(x_ref, mod_ref, g_ref, o_ref, rinv_scr, up_scr):
    n, d = x_ref.shape[1], x_ref.shape[2]
    ng = len(POOL_WINDOWS)
    gw = d // ng
    pad = POOL_PAD
    m = mod_ref[0]
    x = x_ref[0]
    rinv_scr[...] = lax.rsqrt(jnp.mean(x * x, axis=-1, keepdims=True) + EPS)
    up_scr[0:pad, :] = jnp.zeros((pad, gw), F32)
    up_scr[pad + n:pad + n + pad, :] = jnp.zeros((pad, gw), F32)
    t = lax.broadcasted_iota(jnp.int32, (n, 1), 0)
    for gi, w in enumerate(POOL_WINDOWS):
        cols = slice(gi * gw, (gi + 1) * gw)
        u = x_ref[0, :, cols] * rinv_scr[...] * g_ref[:, cols] * (1.0 + m[1:2, cols]) + m[0:1, cols]
        up_scr[pad:pad + n, :] = u
        lo, hi = w // 2, w - 1 - w // 2
        tot = up_scr[pad - lo:pad - lo + n, :]
        for dlt in range(-lo + 1, hi + 1):
            tot = tot + up_scr[pad + dlt:pad + dlt + n, :]
        cnt = (jnp.minimum(t + hi + 1, n) - jnp.maximum(t - lo, 0)).astype(F32)
        o_ref[0, :, cols] = (tot / cnt - u).astype(o_ref.dtype)


def _pool_core(x, mod, g):
    bx, n, d = x.shape
    gw = d // len(POOL_WINDOWS)
    return pl.pallas_call(
        _pool_kernel,
        grid=(bx,),
        in_specs=[pl.BlockSpec((1, n, d), lambda b: (b, 0, 0)),
                  pl.BlockSpec((1, N_MOD, d), lambda b: (b, 0, 0)),
                  pl.BlockSpec((1, d), lambda b: (0, 0))],
        out_specs=pl.BlockSpec((1, n, d), lambda b: (b, 0, 0)),
        out_shape=jax.ShapeDtypeStruct((bx, n, d), BF16),
        scratch_shapes=[pltpu.VMEM((n, 1), F32), pltpu.VMEM((n + 2 * POOL_PAD, gw), F32)],
        compiler_params=_cparams(("parallel",)),
        name="pool_core",
    )(x, mod, g.reshape(1, d))


def _unit_tri_inverses(mats):
    c = mats[0].shape[0]
    ii = lax.broadcasted_iota(jnp.int32, (c, c), 0)
    jj = lax.broadcasted_iota(jnp.int32, (c, c), 1)
    blk = lambda t, k: lax.shift_right_logical(t, k)
    eye = (ii == jj).astype(F32)
    pair = (blk(ii, 1) == blk(jj, 1)) & (ii != jj)
    ds = [eye - jnp.where(pair, a, 0.0) for a in mats]
    k = 1
    while (2 << k) <= c:
        m = (blk(ii, k + 1) == blk(jj, k + 1)) & (blk(ii, k) != blk(jj, k))
        xs = [_dot_t(jnp.where(m, a, 0.0), d) for a, d in zip(mats, ds)]
        ys = [_dot_t(d, x) for d, x in zip(ds, xs)]
        ds = [d - y for d, y in zip(ds, ys)]
        k += 1
    return ds


def _dot_t(a, b):
    return _dot(a.astype(BF16), b.astype(BF16))


def _cumsum_rows(tri, x):
    hi = x.astype(BF16)
    lo = (x - hi.astype(F32)).astype(BF16)
    r = _dot(tri, jnp.concatenate([hi, lo], axis=1))
    return r[:, 0:LANES] + r[:, LANES:2 * LANES]


DN_HEADS_PER_STEP = 2


def _dn_kernel(alog_ref, dtb_ref, ng_ref, cw_ref_q, cw_ref_k, cw_ref_v,
                ql_ref, kl_ref, vl_ref, gfl_ref, gbl_ref, gl_ref,
                qc_ref, kc_ref, vc_ref, gfc_ref, gbc_ref, gc_ref,
                yl_ref, *rest, with_ctx_out):
    if with_ctx_out:
        yc_ref, rest = rest[0], rest[1:]
    else:
        yc_ref = None
    q_scr, k_scr, v_scr, g_scr, xp_scr, sf_scr, sb_scr, yacc_scr = rest[:8]
    fwd_bufs, bwd_bufs = rest[8:12], rest[12:16]
    c = DN_CHUNK
    nh = DN_HEADS
    hp = DN_HEADS_PER_STEP
    l = ql_ref.shape[1]
    lc = qc_ref.shape[1]
    dk = ql_ref.shape[2] // hp
    hg = pl.program_id(1)
    kw = DN_CONV_W
    cpad = 8
    n_c, n_l = lc // c, l // c
    nck = n_c + n_l
    heads =[(p, slice(p * dk, (p + 1) * dk)) for p in range(hp)]

    def conv_seq(src_ref, cw_ref, dst_scr, base, n, norm, scale):
        xp_scr[0:cpad, :] = jnp.zeros((cpad, hp * dk), F32)
        xp_scr[cpad:cpad + n, :] = src_ref[0].astype(F32)
        xp_scr[cpad + n:cpad + n + cpad, :] = jnp.zeros((cpad, hp * dk), F32)
        rb = _pick_tile(n, 128)
        for start in range(0, n, rb):
            y = None
            for t in range(kw):
                o = start + cpad - kw // 2 + t
                term = xp_scr[o:o + rb, :] * cw_ref[t:t + 1, :]
                y = term if y is None else y + term
            y = _silu(y)
            if norm:
                y = jnp.concatenate([y[:, hs] * (lax.rsqrt(jnp.sum(y[:, hs] * y[:, hs], axis=-1, keepdims=True) + EPS)
                                                 * scale) for _, hs in heads], axis=1)
            dst_scr[base + start:base + start + rb, :] = y.astype(BF16)

    for src_c, src_l, cw, dst, norm, scale in ((qc_ref, ql_ref, cw_ref_q, q_scr, True, dk ** -0.5),
                                               (kc_ref, kl_ref, cw_ref_k, k_scr, True, 1.0),
                                               (vc_ref, vl_ref, cw_ref_v, v_scr, False, 1.0)):
        conv_seq(src_c, cw, dst, 0, lc, norm, scale)
        conv_seq(src_l, cw, dst, lc, l, norm, scale)
    g_scr[0:lc, :] = gc_ref[0]
    g_scr[lc:lc + l, :] = gl_ref[0]

    lane = lax.broadcasted_iota(jnp.int32, (1, LANES), 1)
    neg_a = -jnp.exp(alog_ref[...])
    dtb = dtb_ref[...]
    ii = lax.broadcasted_iota(jnp.int32, (c, c), 0)
    jj = lax.broadcasted_iota(jnp.int32, (c, c), 1)
    tri_l = (ii >= jj).astype(BF16)
    tri_u = (ii <= jj).astype(BF16)

    def lane_col(x, idx):
        return jnp.broadcast_to(jnp.sum(jnp.where(lane == idx, x, 0.0), axis=1, keepdims=True), (c, LANES))

    def prep_group(cis):
        chains = []
        for ci in cis:
            srows = pl.ds(pl.multiple_of(ci * c, c), c)
            graw = g_scr[srows, :]
            sig = jax.nn.sigmoid(graw)
            la_all = neg_a * jax.nn.softplus(graw + dtb)
            for p, hs in heads:
                h = hg * hp + p
                q, k = q_scr[srows, hs], k_scr[srows, hs]
                qf, kf, vf = q.astype(F32), k.astype(F32), v_scr[srows, hs].astype(F32)
                kk, qk = _dot_nt(k, k), _dot_nt(q, k)
                for upper, bufs in ((False, fwd_bufs), (True, bwd_bufs)):
                    beta = lane_col(sig, h + (nh if upper else 0))
                    la = lane_col(la_all, h + (3 * nh if upper else 2 * nh))
                    chains.append(dict(upper=upper, bufs=bufs, ci=ci, p=p, hs=hs, srows=srows, qf=qf, kf=kf, vf=vf,
                                       kk=kk, qk=qk, beta=beta, la=la))
        for ch in chains:
            ch["g"] = _cumsum_rows(tri_u if ch["upper"] else tri_l, ch["la"])
        for ch in chains:
            g = ch["g"]
            keep = (ii <= jj) if ch["upper"] else (ii >= jj)
            g_row = g.T[0:c, 0:c]
            ch["decay"] = jnp.where(keep, jnp.exp(jnp.where(keep, g[:, 0:c] - g_row, 0.0)), 0.0)
        t_invs = _unit_tri_inverses([jnp.where(ii != jj, ch["kk"] * ch["beta"][:, 0:c] * ch["decay"], 0.0)
                                     for ch in chains])
        sols = []
        for ch, t_inv in zip(chains, t_invs):
            ch["eg"] = jnp.exp(ch["g"])
            rhs = jnp.concatenate([ch["vf"] * ch["beta"], ch["kf"] * (ch["beta"] * ch["eg"])], axis=1).astype(BF16)
            sols.append(_dot(t_inv.astype(BF16), rhs))
        for ch, sol in zip(chains, sols):
            g = ch["g"]
            ch["g_last"] = g[0:1, :] if ch["upper"] else g[c - 1:c, :]
            ch["uw"] = sol.astype(BF16)
            ch["kgt"] = (ch["kf"] * jnp.exp(ch["g_last"] - g)).T.astype(BF16)
            ch["attn"] = (ch["qk"] * ch["decay"]).astype(BF16)
        kus = [_dot(ch["kgt"], ch["uw"]) for ch in chains]
        aus = [_dot(ch["attn"], ch["uw"]) for ch in chains]
        for ch, ku, au in zip(chains, kus, aus):
            mq_scr, n_scr, o_scr, gl_scr = ch["bufs"]
            ci, p, hs = ch["ci"], ch["p"], ch["hs"]
            mq_scr[pl.ds(pl.multiple_of(ci * (dk + c), c), dk), hs] = ku[:, dk:2 * dk].astype(BF16)
            mq_scr[pl.ds(pl.multiple_of(ci * (dk + c) + dk, c), c), hs] = (ch["qf"] * ch["eg"] - au[:, dk:2 * dk]).astype(BF16)
            n_scr[pl.ds(pl.multiple_of(ci * dk, dk), dk), hs] = ku[:, 0:dk].astype(BF16)
            o_scr[ch["srows"], hs] = au[:, 0:dk]
            gl_scr[p, pl.ds(pl.multiple_of(ci * 8, 8), 8), :] = jnp.broadcast_to(jnp.exp(ch["g_last"]), (8, LANES))

    nb = next(t for t in (9, 6, 4, 2, 1) if nck % t == 0)

    def prep_body(r, carry):
        prep_group([r * nb + t for t in range(nb)])
        return carry

    lax.fori_loop(0, nck // nb, prep_body, 0)

    sf_scr[...] = jnp.zeros_like(sf_scr)
    sb_scr[...] = jnp.zeros_like(sb_scr)
    norm_g = ng_ref[...]

    def step_all(base, chunk_ids, gate_refs):
        chains = []
        for ci, gate_ref, s_scr, bufs in zip(chunk_ids, gate_refs, (sf_scr, sb_scr), (fwd_bufs, bwd_bufs)):
            gci = base // c + ci
            for p, hs in heads:
                chains.append(dict(rows=pl.ds(pl.multiple_of(ci * c, c), c), gci=gci, gate_ref=gate_ref, s_scr=s_scr,
                                   bufs=bufs, p=p, hs=hs, srows=pl.ds(pl.multiple_of(gci * c, c), c)))
        for ch in chains:
            mq_scr, n_scr, o_scr, gl_scr = ch["bufs"]
            ch["s"] = ch["s_scr"][ch["p"]]
            mq = mq_scr[pl.ds(pl.multiple_of(ch["gci"] * (dk + c), c), dk + c), ch["hs"]]
            ch["r"] = _dot(mq, ch["s"].astype(BF16))
        for ch in chains:
            mq_scr, n_scr, o_scr, gl_scr = ch["bufs"]
            gl = gl_scr[ch["p"], pl.ds(pl.multiple_of(ch["gci"] * 8, 8), 8), :]
            incr = n_scr[pl.ds(pl.multiple_of(ch["gci"] * dk, dk), dk), ch["hs"]].astype(F32)
            ch["s_scr"][ch["p"]] = ch["s"] * gl[0:1, :] + (incr - ch["r"][0:dk, :])
        outs = []
        for ch in chains:
            if ch["gate_ref"] is None:
                outs.append(None)
                continue
            o = ch["r"][dk:dk + c, :] + ch["bufs"][2][ch["srows"], ch["hs"]]
            y = o * lax.rsqrt(jnp.mean(o * o, axis=-1, keepdims=True) + EPS) * norm_g
            outs.append(y * _silu(ch["gate_ref"][0, ch["rows"], ch["hs"]].astype(F32)))
        return chains, outs

    def make_step(base, gf_ref, gb_ref, y_ref):
        def step(cf, cb, second_half):
            chains, outs = step_all(base, (cf, cb), (gf_ref, gb_ref))
            if second_half is None:
                return
            for ch, y in zip(chains, outs):
                if second_half:
                    y_ref[0, ch["rows"], ch["hs"]] = (yacc_scr[ch["rows"], ch["hs"]] + y).astype(y_ref.dtype)
                else:
                    yacc_scr[ch["rows"], ch["hs"]] = y
        return step

    if with_ctx_out:
        _run_bidirectional(n_c, make_step(0, gfc_ref, gbc_ref, yc_ref), True)
    else:
        _run_bidirectional(n_c, make_step(0, None, None, None), False)
    _run_bidirectional(n_l, make_step(lc, gfl_ref, gbl_ref, yl_ref), True)


def _deltanet_core(p_lat, g_lat, p_ctx, g_ctx, conv_w, a_lanes, dtb_lanes, norm_g, with_ctx_out):
    b, l, _ = p_lat.shape
    lc = p_ctx.shape[1]
    nh = DN_HEADS
    hp = DN_HEADS_PER_STEP
    ng = nh // hp
    dk = norm_g.shape[-1]
    w = hp * dk

    def specs(n):
        sp = [pl.BlockSpec((1, n, w), lambda bi, h, k=k: (bi, 0, k * ng + h)) for k in range(5)]
        return sp + [pl.BlockSpec((1, n, LANES), lambda bi, h: (bi, 0, 0))]

    const2 = lambda bi, h: (0, 0)
    kw = conv_w.shape[0]
    c = DN_CHUNK
    ltot = lc + l
    nck = ltot // c
    dir_bufs = [pltpu.VMEM((nck * (dk + c), w), BF16), pltpu.VMEM((nck * dk, w), BF16), pltpu.VMEM((ltot, w), F32),
                pltpu.VMEM((hp, nck * 8, LANES), F32)]
    out_specs = [pl.BlockSpec((1, l, w), lambda bi, h: (bi, 0, h))]
    out_shape = [jax.ShapeDtypeStruct((b, l, nh * dk), BF16)]
    if with_ctx_out:
        out_specs.append(pl.BlockSpec((1, lc, w), lambda bi, h: (bi, 0, h)))
        out_shape.append(jax.ShapeDtypeStruct((b, lc, nh * dk), BF16))
    outs = pl.pallas_call(
        functools.partial(_dn_kernel, with_ctx_out=with_ctx_out),
        grid=(b, ng),
        in_specs=[pl.BlockSpec((1, LANES), const2), pl.BlockSpec((1, LANES), const2), pl.BlockSpec((1, dk), const2)]
                 + [pl.BlockSpec((kw, w), lambda bi, h, k=k: (0, k * ng + h)) for k in range(3)]
                 + specs(l) + specs(lc),
        out_specs=out_specs, out_shape=out_shape,
        scratch_shapes=[pltpu.VMEM((ltot, w), BF16), pltpu.VMEM((ltot, w), BF16), pltpu.VMEM((ltot, w), BF16),
                        pltpu.VMEM((ltot, LANES), F32), pltpu.VMEM((max(l, lc) + 16, w), F32),
                        pltpu.VMEM((hp, dk, dk), F32), pltpu.VMEM((hp, dk, dk), F32),
                        pltpu.VMEM((max(l, lc), w), F32)] + 2 * dir_bufs,
        compiler_params=pltpu.CompilerParams(dimension_semantics=("parallel", "parallel"),
                                             vmem_limit_bytes=DN_VMEM_LIMIT),
        name="deltanet_core",
    )(a_lanes, dtb_lanes, norm_g.reshape(1, dk), conv_w, conv_w, conv_w,
      *([p_lat] * 5), g_lat, *([p_ctx] * 5), g_ctx)
    return (outs[0], outs[1]) if with_ctx_out else (outs[0], None)


def kernel(x, c, ctx, c_ctx, ada_w, ada_b, mix_pre_g, mix_post_g, mlp_pre_g, mlp_post_g, mlp_w1, mlp_w2, ret_w_in, ret_decay_logit, ret_w_out, att_w_in, att_sink, att_w_out, pool_w, pool_b, pool_scale, dn_w_in, dn_conv_w, dn_a_log, dn_dt_bias, dn_norm_g, dn_w_out):
    b, l, d = x.shape
    lc = ctx.shape[1]
    depth = ada_w.shape[0]
    n_mixers = 4

    rows = -(-(b + 1) // 8) * 8
    cond = jnp.zeros((rows, d), F32).at[:b].set(c).at[b].set(c_ctx)
    mods = _modulation(cond, ada_w, ada_b)
    xc = ctx.reshape(1, b * lc, d)

    for i in range(depth):
        kind, inst = i % n_mixers, i // n_mixers
        need_ctx = i < depth - 1
        m_lat = mods[i, :b].reshape(b, N_MOD, d)
        m_ctx = mods[i, b:b + 1].reshape(1, N_MOD, d)
        w1, w2 = mlp_w1[i].astype(BF16), mlp_w2[i].astype(BF16)
        post_kw = {}
        if kind == 0:
            w_in = ret_w_in[inst].astype(BF16)
            p_lat = _inproj(x, m_lat, mix_pre_g[i], w_in, name="ret_inproj")
            p_ctx = _inproj(xc, m_ctx, mix_pre_g[i], w_in, name="ret_inproj_ctx").reshape(b, lc, -1)
            dk = d // RET_HEADS
            cos, sin = _rope_tables(l, dk, 1)
            a_lat, a_ctx = _retention_core(p_lat, p_ctx, ret_decay_logit[inst], cos, sin)
            w_out = ret_w_out[inst].astype(BF16)
        elif kind == 1:
            grp = ATT_HEADS // ATT_KV_HEADS
            dh = d // ATT_HEADS
            nq, nkv = ATT_HEADS * dh, ATT_KV_HEADS * dh
            wq = att_w_in[inst][:, :nq].reshape(d, ATT_KV_HEADS, grp * dh)
            wk = att_w_in[inst][:, nq:nq + nkv].reshape(d, ATT_KV_HEADS, dh)
            wv = att_w_in[inst][:, nq + nkv:].reshape(d, ATT_KV_HEADS, dh)
            w_in = jnp.concatenate([wq, wk, wv], axis=2).reshape(d, nq + 2 * nkv).astype(BF16)
            p_lat = _inproj(x, m_lat, mix_pre_g[i], w_in, name="att_inproj")
            p_ctx = _inproj(xc, m_ctx, mix_pre_g[i], w_in, name="att_inproj_ctx").reshape(b, lc, -1)
            cos_h, sin_h = _rope_tables(l, dh, 1)
            one, zero = jnp.ones((l, dh), F32), jnp.zeros((l, dh), F32)
            qs = dh ** -0.5
            cos = jnp.concatenate([jnp.tile(cos_h, (1, grp)) * qs, cos_h, one], axis=1)
            sin = jnp.concatenate([jnp.tile(sin_h, (1, grp)) * qs, sin_h, zero], axis=1)
            a_lat, a_ctx = _attention_core(p_lat, p_ctx, att_sink[inst], cos, sin, need_ctx)
            w_out = att_w_out[inst].astype(BF16)
        elif kind == 2:
            a_lat = _pool_core(x, m_lat, mix_pre_g[i])
            a_ctx = _pool_core(xc.reshape(b, lc, d), jnp.broadcast_to(m_ctx, (b, N_MOD, d)), mix_pre_g[i]) if need_ctx else None
            w_out = pool_w[inst].astype(BF16)
            post_kw = dict(pool_bias=pool_b[inst], pool_scale=pool_scale[inst])
        else:
            nh = DN_HEADS
            dk = d // nh
            nqkv = 3 * nh * dk
            wd = dn_w_in[inst]
            w_main = jnp.concatenate([wd[:, :nqkv], wd[:, nqkv + 4 * nh:]], axis=1).astype(BF16)
            w_gate = jnp.zeros((d, LANES), F32).at[:, :4 * nh].set(wd[:, nqkv:nqkv + 4 * nh]).astype(BF16)
            p_lat, g_lat = _inproj(x, m_lat, mix_pre_g[i], w_main, aux_w=w_gate, name="dn_inproj")
            p_ctx, g_ctx = _inproj(xc, m_ctx, mix_pre_g[i], w_main, aux_w=w_gate, name="dn_inproj_ctx")
            p_ctx, g_ctx = p_ctx.reshape(b, lc, -1), g_ctx.reshape(b, lc, -1)
            a_lanes = jnp.zeros((1, LANES), F32).at[0, 2 * nh:4 * nh].set(dn_a_log[inst].reshape(-1))
            dtb_lanes = jnp.zeros((1, LANES), F32).at[0, 2 * nh:4 * nh].set(dn_dt_bias[inst].reshape(-1))
            a_lat, a_ctx = _deltanet_core(p_lat, g_lat, p_ctx, g_ctx, dn_conv_w[inst], a_lanes, dtb_lanes,
                                          dn_norm_g[inst], need_ctx)
            w_out = dn_w_out[inst].astype(BF16)

        x = _post(a_lat, x, m_lat, mix_post_g[i], mlp_pre_g[i], mlp_post_g[i], w_out, w1, w2,
                  name="post", **post_kw)
        if need_ctx:
            xc = _post(a_ctx.reshape(1, b * lc, -1), xc, m_ctx, mix_post_g[i], mlp_pre_g[i], mlp_post_g[i],
                       w_out, w1, w2, name="post_ctx", **post_kw)
    return x
```
